```python
import math
import jax, jax.numpy as jnp
from jax import lax
import numpy as np


D_MODEL = 1024
BATCH = 32
SEQ = 2048
DEPTH = 2

N_EVEN = (DEPTH + 1) // 2
N_ODD = DEPTH // 2
HEAD_DIM = 64
ATT_HEADS = D_MODEL // (2 * HEAD_DIM)
ATT_WIDTH = ATT_HEADS * HEAD_DIM
ROPE_DIM = HEAD_DIM // 4
ROPE_THETA = 500000.0
DILATED_BRANCHES = ((128, 1), (512, 4), (2048, 16))
ATT_BLOCK = 128
NEG_INF = -1e30
SSM_WIDTH = D_MODEL - ATT_WIDTH
SSM_GROUP = 16
SSM_GROUPS = SSM_WIDTH // SSM_GROUP
SSM_STATE = 64
EVEN_IN = 3 * ATT_WIDTH + SSM_WIDTH
LRU_WIDTH = D_MODEL
LRU_BLOCKS = 4
LRU_BLOCK_DIM = LRU_WIDTH // LRU_BLOCKS
CONV_WIDTH = 4
RG_C = 8.0
FFN_HIDDEN = -(-8 * D_MODEL // (3 * 256)) * 256
NORM_EPS = 1e-6

kernel_name = 'hybrid_dilated_attn_s5_rglru_block'


def rms_norm(x, g):
    x32 = x.astype(jnp.float32)
    y = x32 * lax.rsqrt(jnp.mean(x32 * x32, axis=-1, keepdims=True) + NORM_EPS) * g.astype(jnp.float32)
    return y.astype(x.dtype)


def swiglu(x, w_gate, w_up, w_down):
    return (jax.nn.silu(x @ w_gate) * (x @ w_up)) @ w_down


def _linear_combine(e1, e2):
    a1, b1 = e1
    a2, b2 = e2
    return a1 * a2, a2 * b1 + b2


def partial_rope(t, positions):
    half = ROPE_DIM // 2
    inv_freq = ROPE_THETA ** (-(jnp.arange(half, dtype=jnp.float32) * 2.0 / ROPE_DIM))
    ang = positions.astype(jnp.float32)[..., None] * inv_freq
    cos = jnp.cos(ang)[:, :, None, :]
    sin = jnp.sin(ang)[:, :, None, :]
    t1 = t[..., :half]
    t2 = t[..., half:ROPE_DIM]
    return jnp.concatenate([t1 * cos - t2 * sin, t2 * cos + t1 * sin, t[..., ROPE_DIM:]], axis=-1)


def _dilated_branch(q, k, v, window, dilation):
    b, s, h, c = q.shape
    L = s // dilation
    nb = -(-L // ATT_BLOCK)
    Lp = nb * ATT_BLOCK
    span = window // dilation

    def strided(t):
        t = t.reshape(b, L, dilation, h, c).transpose(0, 2, 1, 3, 4)
        return jnp.pad(t, ((0, 0), (0, 0), (0, Lp - L), (0, 0), (0, 0)))

    def band(t):
        t = jnp.pad(strided(t), ((0, 0), (0, 0), (ATT_BLOCK, 0), (0, 0), (0, 0)))
        t = t.reshape(b, dilation, nb + 1, ATT_BLOCK, h, c)
        return jnp.concatenate([t[:, :, :-1], t[:, :, 1:]], axis=3)

    qb = strided(q).reshape(b, dilation, nb, ATT_BLOCK, h, c)
    kb = band(k)
    vb = band(v)
    scores = jnp.einsum('brnqhc,brnkhc->brnhqk', qb, kb)
    qi = jnp.arange(ATT_BLOCK)[:, None] + ATT_BLOCK
    ki = jnp.arange(2 * ATT_BLOCK)[None, :]
    dist = qi - ki
    kpos = jnp.arange(nb)[:, None, None] * ATT_BLOCK + ki[None] - ATT_BLOCK
    mask = ((dist >= 0) & (dist <= span))[None] & (kpos >= 0)
    scores = jnp.where(mask[None, None, :, None], scores, NEG_INF)
    m = jnp.max(scores, axis=-1)
    p = jnp.exp(scores - m[..., None])
    l = jnp.sum(p, axis=-1)
    acc = jnp.einsum('brnhqk,brnkhc->brnqhc', p, vb)

    def unstride(t):
        t = t.reshape((b, dilation, Lp) + t.shape[4:])[:, :, :L]
        t = jnp.moveaxis(t, 1, 2)
        return t.reshape((b, s) + t.shape[3:])

    m = jnp.moveaxis(m, -1, 3)
    l = jnp.moveaxis(l, -1, 3)
    return unstride(acc), unstride(m), unstride(l)


def dilated_attention(q, k, v):
    outs = [_dilated_branch(q, k, v, w, d) for (w, d) in DILATED_BRANCHES]
    acc_all = jnp.stack([o[0] for o in outs])
    m_all = jnp.stack([o[1] for o in outs])
    l_all = jnp.stack([o[2] for o in outs])
    wts = jnp.exp(m_all - jnp.max(m_all, axis=0, keepdims=True))
    num = jnp.einsum('ibsh,ibshc->bshc', wts, acc_all)
    den = jnp.sum(wts * l_all, axis=0)
    return num / den[..., None]


def s5_mixer(u, a_re, a_im, log_dt, b_re, b_im, c_re, c_im, d_skip, w_glu):
    b, s, _ = u.shape
    u = u.reshape(b, s, SSM_GROUPS, SSM_GROUP)
    f32 = jnp.float32
    A = lax.complex(a_re.astype(f32), a_im.astype(f32))
    dt = jnp.exp(log_dt.astype(f32))[:, None]
    a_bar = jnp.exp(A * dt)
    b_bar = ((a_bar - 1.0) / A)[..., None] * lax.complex(b_re.astype(f32), b_im.astype(f32))
    bu = jnp.einsum('gpc,bsgc->bsgp', b_bar, u.astype(jnp.complex64))
    a_seq = jnp.broadcast_to(a_bar, (s,) + a_bar.shape)
    h = jax.vmap(lambda bu_b: lax.associative_scan(_linear_combine, (a_seq, bu_b), axis=0)[1])(bu)
    c_mat = lax.complex(c_re.astype(f32), c_im.astype(f32))
    y = jnp.einsum('gcp,bsgp->bsgc', c_mat, h).real + d_skip.astype(f32) * u
    y = jax.nn.gelu(y.reshape(b, s, SSM_WIDTH))
    return y * jax.nn.sigmoid(y @ w_glu.astype(f32))


def rglru_mixer(xb, conv_w, conv_b, w_r, b_r, w_i, b_i, lam):
    b, s, r = xb.shape
    f32 = jnp.float32
    xc = lax.conv_general_dilated(xb, conv_w.astype(f32)[:, None, :], window_strides=(1,),
                                  padding=[(CONV_WIDTH - 1, 0)],
                                  dimension_numbers=('NWC', 'WIO', 'NWC'),
                                  feature_group_count=r) + conv_b.astype(f32)
    xh = xc.reshape(b, s, LRU_BLOCKS, LRU_BLOCK_DIM)
    gate_r = jax.nn.sigmoid(jnp.einsum('bshi,hij->bshj', xh, w_r.astype(f32)).reshape(b, s, r) + b_r.astype(f32))
    gate_i = jax.nn.sigmoid(jnp.einsum('bshi,hij->bshj', xh, w_i.astype(f32)).reshape(b, s, r) + b_i.astype(f32))
    log_a = -RG_C * gate_r * jax.nn.softplus(-lam.astype(f32))
    a = jnp.exp(log_a)
    mult = jnp.sqrt(-jnp.expm1(2.0 * log_a))
    _, h = lax.associative_scan(_linear_combine, (a, mult * (gate_i * xc)), axis=1)
    return h


def even_mixer(xn, positions, w_in, w_out, a_re, a_im, log_dt, b_re, b_im, c_re, c_im, d_skip, w_glu):
    b, s, _ = xn.shape
    z = (xn @ w_in).astype(jnp.float32)
    q = z[..., :ATT_WIDTH].reshape(b, s, ATT_HEADS, HEAD_DIM)
    k = z[..., ATT_WIDTH:2 * ATT_WIDTH].reshape(b, s, ATT_HEADS, HEAD_DIM)
    v = z[..., 2 * ATT_WIDTH:3 * ATT_WIDTH].reshape(b, s, ATT_HEADS, HEAD_DIM)
    u = z[..., 3 * ATT_WIDTH:]
    q = partial_rope(q, positions) * (HEAD_DIM ** -0.5)
    k = partial_rope(k, positions)
    att = dilated_attention(q, k, v).reshape(b, s, ATT_WIDTH)
    ssm = s5_mixer(u, a_re, a_im, log_dt, b_re, b_im, c_re, c_im, d_skip, w_glu)
    y = jnp.concatenate([att, ssm], axis=-1).astype(w_out.dtype)
    return (y @ w_out).astype(xn.dtype)


def odd_mixer(xn, w_in, w_out, conv_w, conv_b, w_r, b_r, w_i, b_i, lam):
    z = (xn @ w_in).astype(jnp.float32)
    h = rglru_mixer(z[..., :LRU_WIDTH], conv_w, conv_b, w_r, b_r, w_i, b_i, lam)
    y = (h * jax.nn.gelu(z[..., LRU_WIDTH:])).astype(w_out.dtype)
    return (y @ w_out).astype(xn.dtype)


def setup_inputs(seed: int = 0) -> dict:
    key = jax.random.key(seed)
    ks = iter(jax.random.split(key, 32))
    f32 = jnp.float32

    def nrm(shape, scale):
        return jax.random.normal(next(ks), shape, f32) * scale

    G, P, NC, BD = SSM_GROUPS, SSM_STATE, SSM_GROUP, LRU_BLOCK_DIM
    x = nrm((BATCH, SEQ, D_MODEL), 1.0)
    offset = jax.random.randint(next(ks), (BATCH, 1), 0, SEQ, dtype=jnp.int32)
    positions = offset + jnp.arange(SEQ, dtype=jnp.int32)[None, :]
    norm_mix_pre = 1.0 + nrm((DEPTH, D_MODEL), 0.05)
    norm_mix_post = 1.0 + nrm((DEPTH, D_MODEL), 0.05)
    norm_ffn_pre = 1.0 + nrm((DEPTH, D_MODEL), 0.05)
    norm_ffn_post = 1.0 + nrm((DEPTH, D_MODEL), 0.05)
    ev_w_in = nrm((N_EVEN, D_MODEL, EVEN_IN), D_MODEL ** -0.5)
    ev_w_out = nrm((N_EVEN, D_MODEL, D_MODEL), D_MODEL ** -0.5)
    s5_a_re = -0.5 * jnp.exp(nrm((N_EVEN, G, P), 0.02))
    s5_a_im = jnp.pi * jnp.arange(P, dtype=f32) + nrm((N_EVEN, G, P), 0.02)
    s5_log_dt = jax.random.uniform(next(ks), (N_EVEN, G), f32, math.log(1e-3), math.log(1e-1))
    s5_b_re = nrm((N_EVEN, G, P, NC), (2 * NC) ** -0.5)
    s5_b_im = nrm((N_EVEN, G, P, NC), (2 * NC) ** -0.5)
    s5_c_re = nrm((N_EVEN, G, NC, P), (2 * P) ** -0.5)
    s5_c_im = nrm((N_EVEN, G, NC, P), (2 * P) ** -0.5)
    s5_d = nrm((N_EVEN, G, NC), 1.0)
    s5_w_glu = nrm((N_EVEN, SSM_WIDTH, SSM_WIDTH), SSM_WIDTH ** -0.5)
    od_w_in = nrm((N_ODD, D_MODEL, 2 * LRU_WIDTH), D_MODEL ** -0.5)
    od_w_out = nrm((N_ODD, LRU_WIDTH, D_MODEL), LRU_WIDTH ** -0.5)
    rg_conv_w = nrm((N_ODD, CONV_WIDTH, LRU_WIDTH), CONV_WIDTH ** -0.5)
    rg_conv_b = nrm((N_ODD, LRU_WIDTH), 0.01)
    rg_w_r = nrm((N_ODD, LRU_BLOCKS, BD, BD), BD ** -0.5)
    rg_b_r = nrm((N_ODD, LRU_WIDTH), 0.01)
    rg_w_i = nrm((N_ODD, LRU_BLOCKS, BD, BD), BD ** -0.5)
    rg_b_i = nrm((N_ODD, LRU_WIDTH), 0.01)
    a0 = jax.random.uniform(next(ks), (N_ODD, LRU_WIDTH), f32, 0.9, 0.999)
    sig = a0 ** (1.0 / RG_C)
    rg_lam = jnp.log(sig) - jnp.log1p(-sig)
    ffn_w_gate = nrm((DEPTH, D_MODEL, FFN_HIDDEN), D_MODEL ** -0.5)
    ffn_w_up = nrm((DEPTH, D_MODEL, FFN_HIDDEN), D_MODEL ** -0.5)
    ffn_w_down = nrm((DEPTH, FFN_HIDDEN, D_MODEL), FFN_HIDDEN ** -0.5)
    return {'x': x, 'positions': positions,
            'norm_mix_pre': norm_mix_pre, 'norm_mix_post': norm_mix_post,
            'norm_ffn_pre': norm_ffn_pre, 'norm_ffn_post': norm_ffn_post,
            'ev_w_in': ev_w_in, 'ev_w_out': ev_w_out,
            's5_a_re': s5_a_re, 's5_a_im': s5_a_im, 's5_log_dt': s5_log_dt,
            's5_b_re': s5_b_re, 's5_b_im': s5_b_im, 's5_c_re': s5_c_re, 's5_c_im': s5_c_im,
            's5_d': s5_d, 's5_w_glu': s5_w_glu,
            'od_w_in': od_w_in, 'od_w_out': od_w_out,
            'rg_conv_w': rg_conv_w, 'rg_conv_b': rg_conv_b,
            'rg_w_r': rg_w_r, 'rg_b_r': rg_b_r, 'rg_w_i': rg_w_i, 'rg_b_i': rg_b_i, 'rg_lam': rg_lam,
            'ffn_w_gate': ffn_w_gate, 'ffn_w_up': ffn_w_up, 'ffn_w_down': ffn_w_down}


def reference(x, positions, norm_mix_pre, norm_mix_post, norm_ffn_pre, norm_ffn_post,
              ev_w_in, ev_w_out, s5_a_re, s5_a_im, s5_log_dt, s5_b_re, s5_b_im, s5_c_re, s5_c_im,
              s5_d, s5_w_glu, od_w_in, od_w_out, rg_conv_w, rg_conv_b, rg_w_r, rg_b_r, rg_w_i, rg_b_i,
              rg_lam, ffn_w_gate, ffn_w_up, ffn_w_down):
    for layer in range(DEPTH):
        hn = rms_norm(x, norm_mix_pre[layer])
        if layer % 2 == 0:
            e = layer // 2
            mix = even_mixer(hn, positions, ev_w_in[e], ev_w_out[e], s5_a_re[e], s5_a_im[e],
                             s5_log_dt[e], s5_b_re[e], s5_b_im[e], s5_c_re[e], s5_c_im[e],
                             s5_d[e], s5_w_glu[e])
        else:
            o = layer // 2
            mix = odd_mixer(hn, od_w_in[o], od_w_out[o], rg_conv_w[o], rg_conv_b[o],
                            rg_w_r[o], rg_b_r[o], rg_w_i[o], rg_b_i[o], rg_lam[o])
        x = x + rms_norm(mix, norm_mix_post[layer])
        hn = rms_norm(x, norm_ffn_pre[layer])
        ff = swiglu(hn, ffn_w_gate[layer], ffn_w_up[layer], ffn_w_down[layer])
        x = x + rms_norm(ff, norm_ffn_post[layer])
    return x
```

```python
import functools

import jax
import jax.numpy as jnp
from jax import lax
from jax.experimental import pallas as pl
from jax.experimental.pallas import tpu as pltpu

F32 = jnp.float32
BF16 = jnp.bfloat16

D_MODEL = 1024
HEAD_DIM = 64
ATT_WIDTH = 512
ROPE_DIM = 16
ROPE_THETA = 500000.0
ATT_BLOCK = 128
NEG_INF = -1e30
SSM_WIDTH = 512
SSM_GROUP = 16
SSM_GROUPS = 32
SSM_STATE = 64
LRU_WIDTH = 1024
LRU_BLOCKS = 4
LRU_BLOCK_DIM = 256
CONV_WIDTH = 4
RG_C = 8.0
FFN_HIDDEN = 2816
NORM_EPS = 1e-6

LANES = 128
SUBLANES = 8
N_SLABS = ATT_WIDTH // LANES
GROUPS_PER_SLAB = LANES // SSM_GROUP
SLAB_STATE = GROUPS_PER_SLAB * SSM_STATE
VMEM_LIMIT_BYTES = 56 * 1024 * 1024


def _rms_norm(x, g):
    return x * lax.rsqrt(jnp.mean(x * x, axis=-1, keepdims=True) + NORM_EPS) * g


def _const_spec(shape):
    n = len(shape)
    return pl.BlockSpec(shape, lambda *_: (0,) * n, pipeline_mode=pl.Buffered(1))


def _params(*sem):
    return pltpu.CompilerParams(dimension_semantics=sem, vmem_limit_bytes=VMEM_LIMIT_BYTES)


def _even_in_kernel(x_ref, pos_ref, g_ref, w_ref, invf_ref, sgn_ref, q_ref, k_ref, v_ref, u_ref):
    hn = _rms_norm(x_ref[0], g_ref[...])
    z = jnp.dot(hn.astype(BF16), w_ref[...], preferred_element_type=F32)
    ang = pos_ref[0].astype(F32) * invf_ref[...]
    cos = jnp.cos(ang)
    sin = jnp.sin(ang) * sgn_ref[...]
    lane = lax.broadcasted_iota(jnp.int32, (1, LANES), 1) % HEAD_DIM
    first_half = lane < (ROPE_DIM // 2)

    def rope(t):
        partner = jnp.where(first_half, pltpu.roll(t, LANES - ROPE_DIM // 2, 1),
                            pltpu.roll(t, ROPE_DIM // 2, 1))
        return t * cos + partner * sin

    for j in range(N_SLABS):
        lo = j * LANES
        q_ref[0, j] = rope(z[:, lo:lo + LANES]) * (HEAD_DIM ** -0.5)
        k_ref[0, j] = rope(z[:, ATT_WIDTH + lo:ATT_WIDTH + lo + LANES])
        v_ref[0, j] = z[:, 2 * ATT_WIDTH + lo:2 * ATT_WIDTH + lo + LANES]
    u_ref[0] = z[:, 3 * ATT_WIDTH:]


def _even_in(x, positions, g, w_in, tm):
    b, s, d = x.shape
    half = ROPE_DIM // 2
    inv_freq = ROPE_THETA ** (-(jnp.arange(half, dtype=F32) * 2.0 / ROPE_DIM))
    lane = jnp.arange(LANES) % HEAD_DIM
    invf = jnp.where(lane < ROPE_DIM, inv_freq[lane % half], 0.0).astype(F32)[None, :]
    sgn = jnp.where(lane < half, -1.0, jnp.where(lane < ROPE_DIM, 1.0, 0.0)).astype(F32)[None, :]
    slab = jax.ShapeDtypeStruct((b, N_SLABS, s, LANES), F32)
    slab_spec = pl.BlockSpec((1, N_SLABS, tm, LANES), lambda i, j: (i, 0, j, 0))
    return pl.pallas_call(
        _even_in_kernel,
        grid=(b, s // tm),
        in_specs=[
            pl.BlockSpec((1, tm, d), lambda i, j: (i, j, 0)),
            pl.BlockSpec((1, tm, 1), lambda i, j: (i, j, 0)),
            _const_spec((1, d)),
            _const_spec(w_in.shape),
            _const_spec((1, LANES)),
            _const_spec((1, LANES)),
        ],
        out_specs=[slab_spec, slab_spec, slab_spec,
                   pl.BlockSpec((1, tm, SSM_WIDTH), lambda i, j: (i, j, 0))],
        out_shape=[slab, slab, slab, jax.ShapeDtypeStruct((b, s, SSM_WIDTH), F32)],
        compiler_params=_params("parallel", "parallel"),
        name="even_in",
    )(x, positions.reshape(b, s, 1), g, w_in, invf, sgn)


def _attn_kernel(q_ref, k_ref, v_ref, o_ref, acc_ref, m_ref, l_ref, *, seq):
    blk = ATT_BLOCK
    head0 = lax.broadcasted_iota(jnp.int32, (1, LANES), 1) < HEAD_DIM

    def bias(nk):
        qi = lax.broadcasted_iota(jnp.int32, (2 * blk, nk), 0) % blk + (nk - blk)
        ki = lax.broadcasted_iota(jnp.int32, (2 * blk, nk), 1)
        dist = qi - ki
        return jnp.where((dist >= 0) & (dist <= blk), 0.0, NEG_INF).astype(F32)

    biases = {blk: bias(blk), 2 * blk: bias(2 * blk)}

    def unit(qb, kb, vb, nk):
        q2 = jnp.concatenate([jnp.where(head0, qb, 0.0), jnp.where(head0, 0.0, qb)], axis=0).astype(BF16)
        s = lax.dot_general(q2, kb.astype(BF16), (((1,), (1,)), ((), ())), preferred_element_type=F32)
        s = s + biases[nk]
        m = jnp.max(s, axis=-1, keepdims=True)
        p = jnp.exp(s - m)
        l = jnp.sum(p, axis=-1, keepdims=True)
        o = jnp.dot(p.astype(BF16), vb.astype(BF16), preferred_element_type=F32)
        return (jnp.where(head0, o[:blk], o[blk:]),
                jnp.where(head0, m[:blk], m[blk:]),
                jnp.where(head0, l[:blk], l[blk:]))

    def run_branch(br, dil):
        n_blocks = seq // dil // blk

        def rows(start, size):
            if dil == 1:
                return pl.ds(start, size)
            return pl.ds(start, size, stride=dil)

        def store(start, res):
            acc, m, l = res
            acc_ref[br, rows(start, blk), :] = acc
            m_ref[br, rows(start, blk), :] = m
            l_ref[br, rows(start, blk), :] = l

        def first_block(r, carry):
            store(r, unit(q_ref[0, 0, rows(r, blk), :], k_ref[0, 0, rows(r, blk), :],
                          v_ref[0, 0, rows(r, blk), :], blk))
            return carry

        lax.fori_loop(0, dil, first_block, 0)

        if n_blocks > 1:
            def later_block(i, carry):
                r = i // (n_blocks - 1)
                n = i % (n_blocks - 1) + 1
                q0 = r + dil * blk * n
                k0 = q0 - dil * blk
                store(q0, unit(q_ref[0, 0, rows(q0, blk), :], k_ref[0, 0, rows(k0, 2 * blk), :],
                               v_ref[0, 0, rows(k0, 2 * blk), :], 2 * blk))
                return carry

            lax.fori_loop(0, dil * (n_blocks - 1), later_block, 0)

    run_branch(0, 1)
    run_branch(1, 4)
    run_branch(2, 16)

    m0, m1, m2 = m_ref[0], m_ref[1], m_ref[2]
    mx = jnp.maximum(jnp.maximum(m0, m1), m2)
    w0, w1, w2 = jnp.exp(m0 - mx), jnp.exp(m1 - mx), jnp.exp(m2 - mx)
    num = w0 * acc_ref[0] + w1 * acc_ref[1] + w2 * acc_ref[2]
    den = w0 * l_ref[0] + w1 * l_ref[1] + w2 * l_ref[2]
    o_ref[0] = (num / den).astype(o_ref.dtype)


def _attention(q, k, v):
    b, n_slabs, s, _ = q.shape
    spec = pl.BlockSpec((1, 1, s, LANES), lambda i, j: (i, j, 0, 0))
    return pl.pallas_call(
        functools.partial(_attn_kernel, seq=s),
        grid=(b, n_slabs),
        in_specs=[spec, spec, spec],
        out_specs=pl.BlockSpec((1, s, LANES), lambda i, j: (i, 0, j)),
        out_shape=jax.ShapeDtypeStruct((b, s, n_slabs * LANES), BF16),
        scratch_shapes=[pltpu.VMEM((3, s, LANES), F32)] * 3,
        compiler_params=_params("parallel", "parallel"),
        name="dilated_attention",
    )(q, k, v)


def _s5_kernel(u_ref, bm_ref, are_ref, aim_ref, cm_ref, d_ref, o_ref, utb, hs, hcar, ys, *, ts):
    @pl.when(pl.program_id(2) == 0)
    def _():
        hcar[...] = jnp.zeros_like(hcar)

    for b in range(SUBLANES):
        utb[pl.ds(b, ts, stride=SUBLANES), :] = u_ref[b]
    u_tb = utb[...]
    hs[...] = jnp.dot(u_tb.astype(BF16), bm_ref[0], preferred_element_type=F32)
    ar = jnp.broadcast_to(are_ref[0], (SUBLANES, SLAB_STATE))
    ai = jnp.broadcast_to(aim_ref[0], (SUBLANES, SLAB_STATE))

    def step(t, carry):
        hr, hi = carry
        row = pl.multiple_of(t * SUBLANES, SUBLANES)
        xr = hs[pl.ds(row, SUBLANES), :SLAB_STATE]
        xi = hs[pl.ds(row, SUBLANES), SLAB_STATE:]
        nr = ar * hr - ai * hi + xr
        ni = ar * hi + ai * hr + xi
        hs[pl.ds(row, SUBLANES), :SLAB_STATE] = nr
        hs[pl.ds(row, SUBLANES), SLAB_STATE:] = ni
        return nr, ni

    hr, hi = lax.fori_loop(0, ts, step, (hcar[:, :SLAB_STATE], hcar[:, SLAB_STATE:]), unroll=4)
    hcar[:, :SLAB_STATE] = hr
    hcar[:, SLAB_STATE:] = hi

    y = jnp.dot(hs[...].astype(BF16), cm_ref[0], preferred_element_type=F32)
    ys[...] = jax.nn.gelu(y + d_ref[0] * u_tb)
    for b in range(SUBLANES):
        o_ref[b] = ys[pl.ds(b, ts, stride=SUBLANES), :]


def _s5(u, bmat, a_re, a_im, cmat, dskip, ts):
    b, s, w = u.shape
    rows = SUBLANES * ts
    return pl.pallas_call(
        functools.partial(_s5_kernel, ts=ts),
        grid=(b // SUBLANES, N_SLABS, s // ts),
        in_specs=[
            pl.BlockSpec((SUBLANES, ts, LANES), lambda i, j, t: (i, t, j)),
            pl.BlockSpec((1, LANES, 2 * SLAB_STATE), lambda i, j, t: (j, 0, 0)),
            pl.BlockSpec((1, 1, SLAB_STATE), lambda i, j, t: (j, 0, 0)),
            pl.BlockSpec((1, 1, SLAB_STATE), lambda i, j, t: (j, 0, 0)),
            pl.BlockSpec((1, 2 * SLAB_STATE, LANES), lambda i, j, t: (j, 0, 0)),
            pl.BlockSpec((1, 1, LANES), lambda i, j, t: (j, 0, 0)),
        ],
        out_specs=pl.BlockSpec((SUBLANES, ts, LANES), lambda i, j, t: (i, t, j)),
        out_shape=jax.ShapeDtypeStruct((b, s, w), F32),
        scratch_shapes=[
            pltpu.VMEM((rows, LANES), F32),
            pltpu.VMEM((rows, 2 * SLAB_STATE), F32),
            pltpu.VMEM((SUBLANES, 2 * SLAB_STATE), F32),
            pltpu.VMEM((rows, LANES), F32),
        ],
        compiler_params=_params("parallel", "parallel", "arbitrary"),
        name="s5_scan",
    )(u, bmat, a_re, a_im, cmat, dskip)


def _s5_params(a_re, a_im, log_dt, b_re, b_im, c_re, c_im, d_skip):
    a = lax.complex(a_re.astype(F32), a_im.astype(F32))
    dt = jnp.exp(log_dt.astype(F32))[:, None]
    a_bar = jnp.exp(a * dt)
    b_bar = ((a_bar - 1.0) / a)[..., None] * lax.complex(b_re.astype(F32), b_im.astype(F32))
    eye = jnp.eye(GROUPS_PER_SLAB, dtype=F32)

    def block_diag(m):
        g, r, c = m.shape
        m = m.reshape(N_SLABS, GROUPS_PER_SLAB, r, c)
        return jnp.einsum('sgrc,gh->sgrhc', m, eye).reshape(N_SLABS, GROUPS_PER_SLAB * r, GROUPS_PER_SLAB * c)

    b_t = jnp.swapaxes(b_bar, 1, 2)
    bmat = jnp.concatenate([block_diag(jnp.real(b_t)), block_diag(jnp.imag(b_t))], axis=-1)
    c_re_t = jnp.swapaxes(c_re.astype(F32), 1, 2)
    c_im_t = jnp.swapaxes(c_im.astype(F32), 1, 2)
    cmat = jnp.concatenate([block_diag(c_re_t), block_diag(-c_im_t)], axis=1)
    return (bmat.astype(BF16),
            jnp.real(a_bar).reshape(N_SLABS, 1, SLAB_STATE),
            jnp.imag(a_bar).reshape(N_SLABS, 1, SLAB_STATE),
            cmat.astype(BF16),
            d_skip.astype(F32).reshape(N_SLABS, 1, LANES))


def _ffn_block(x, g_pre, w_gate, w_up, w_down, g_post):
    hn = _rms_norm(x, g_pre).astype(BF16)
    gate = jnp.dot(hn, w_gate, preferred_element_type=F32)
    up = jnp.dot(hn, w_up, preferred_element_type=F32)
    act = (jax.nn.silu(gate) * up).astype(BF16)
    ff = jnp.dot(act, w_down, preferred_element_type=F32)
    return x + _rms_norm(ff, g_post)


def _ffn_kernel(x_ref, gpre_ref, wg_ref, wu_ref, wd_ref, gpost_ref, o_ref):
    o_ref[...] = _ffn_block(x_ref[...], gpre_ref[...], wg_ref[...], wu_ref[...], wd_ref[...], gpost_ref[...])


def _ffn(x, g_pre, w_gate, w_up, w_down, g_post, tm):
    t, d = x.shape
    row_spec = pl.BlockSpec((tm, d), lambda i: (i, 0))
    return pl.pallas_call(
        _ffn_kernel,
        grid=(t // tm,),
        in_specs=[row_spec, _const_spec((1, d)), _const_spec(w_gate.shape), _const_spec(w_up.shape),
                  _const_spec(w_down.shape), _const_spec((1, d))],
        out_specs=row_spec,
        out_shape=jax.ShapeDtypeStruct((t, d), F32),
        compiler_params=_params("parallel"),
        name="ffn",
    )(x, g_pre, w_gate, w_up, w_down, g_post)


def _even_post_kernel(att_ref, yg_ref, x_ref, wglu_ref, wout_ref, gpost_ref,
                      gfpre_ref, wg_ref, wu_ref, wd_ref, gfpost_ref, o_ref):
    yg = yg_ref[...]
    glu = jnp.dot(yg.astype(BF16), wglu_ref[...], preferred_element_type=F32)
    ssm = (yg * jax.nn.sigmoid(glu)).astype(BF16)
    mix = (jnp.dot(att_ref[...], wout_ref[:ATT_WIDTH, :], preferred_element_type=F32)
           + jnp.dot(ssm, wout_ref[ATT_WIDTH:, :], preferred_element_type=F32))
    x1 = x_ref[...] + _rms_norm(mix, gpost_ref[...])
    o_ref[...] = _ffn_block(x1, gfpre_ref[...], wg_ref[...], wu_ref[...], wd_ref[...], gfpost_ref[...])


def _even_post_ffn(att, yg, x, w_glu, w_out, g_post, g_fpre, w_gate, w_up, w_down, g_fpost, tm):
    t, d = x.shape
    row = lambda w: pl.BlockSpec((tm, w), lambda i: (i, 0))
    return pl.pallas_call(
        _even_post_kernel,
        grid=(t // tm,),
        in_specs=[row(ATT_WIDTH), row(SSM_WIDTH), row(d), _const_spec(w_glu.shape), _const_spec(w_out.shape),
                  _const_spec((1, d)), _const_spec((1, d)), _const_spec(w_gate.shape),
                  _const_spec(w_up.shape), _const_spec(w_down.shape), _const_spec((1, d))],
        out_specs=row(d),
        out_shape=jax.ShapeDtypeStruct((t, d), F32),
        compiler_params=_params("parallel"),
        name="even_post_ffn",
    )(att, yg, x, w_glu, w_out, g_post, g_fpre, w_gate, w_up, w_down, g_fpost)


def _rglru_kernel(x_ref, gpre_ref, win_ref, convw_ref, convb_ref, wr_ref, br_ref, wi_ref, bi_ref, lam_ref,
                  wout_ref, gpost_ref, o_ref, hn_tb, xprev, hcar, a_s, b_s, mix_tb, *, ts):
    rows = SUBLANES * ts
    n_slabs = LRU_WIDTH // LANES
    tail = (CONV_WIDTH - 1) * SUBLANES

    @pl.when(pl.program_id(1) == 0)
    def _():
        xprev[...] = jnp.zeros_like(xprev)
        hcar[...] = jnp.zeros_like(hcar)

    for b in range(SUBLANES):
        hn = _rms_norm(x_ref[b], gpre_ref[...])
        for j in range(n_slabs):
            hn_tb[j, pl.ds(b, ts, stride=SUBLANES), :] = hn[:, j * LANES:(j + 1) * LANES]
    hn_rows = jnp.concatenate([hn_tb[j] for j in range(n_slabs)], axis=1).astype(BF16)
    z = jnp.dot(hn_rows, win_ref[...], preferred_element_type=F32)
    xb = z[:, :LRU_WIDTH]
    gate_branch = z[:, LRU_WIDTH:]

    xext = jnp.concatenate([xprev[...], xb], axis=0)
    xc = convb_ref[...]
    for kk in range(CONV_WIDTH):
        xc = xc + convw_ref[kk:kk + 1, :] * xext[kk * SUBLANES:kk * SUBLANES + rows]
    xprev[...] = xb[rows - tail:]

    xcb = xc.astype(BF16)

    def block_gates(w_ref, bias_ref):
        parts = [jnp.dot(xcb[:, h * LRU_BLOCK_DIM:(h + 1) * LRU_BLOCK_DIM], w_ref[h],
                         preferred_element_type=F32) for h in range(LRU_BLOCKS)]
        return jax.nn.sigmoid(jnp.concatenate(parts, axis=1) + bias_ref[...])

    gate_r = block_gates(wr_ref, br_ref)
    gate_i = block_gates(wi_ref, bi_ref)
    log_a = -RG_C * gate_r * jax.nn.softplus(-lam_ref[...])
    a = jnp.exp(log_a)
    mult = jnp.sqrt(-jnp.tanh(log_a) * (1.0 + a * a))
    a_s[...] = a
    b_s[...] = mult * (gate_i * xc)

    def step(t, h):
        row = pl.multiple_of(t * SUBLANES, SUBLANES)
        h = a_s[pl.ds(row, SUBLANES), :] * h + b_s[pl.ds(row, SUBLANES), :]
        b_s[pl.ds(row, SUBLANES), :] = h
        return h

    hcar[...] = lax.fori_loop(0, ts, step, hcar[...], unroll=4)

    y = (b_s[...] * jax.nn.gelu(gate_branch)).astype(BF16)
    mix = _rms_norm(jnp.dot(y, wout_ref[...], preferred_element_type=F32), gpost_ref[...])
    for j in range(n_slabs):
        mix_tb[j] = mix[:, j * LANES:(j + 1) * LANES]
    for b in range(SUBLANES):
        mix_b = jnp.concatenate([mix_tb[j, pl.ds(b, ts, stride=SUBLANES), :] for j in range(n_slabs)], axis=1)
        o_ref[b] = x_ref[b] + mix_b


def _rglru(x, g_pre, w_in, conv_w, conv_b, w_r, b_r, w_i, b_i, lam, w_out, g_post, ts):
    b, s, d = x.shape
    rows = SUBLANES * ts
    blk = pl.BlockSpec((SUBLANES, ts, d), lambda i, t: (i, t, 0))
    return pl.pallas_call(
        functools.partial(_rglru_kernel, ts=ts),
        grid=(b // SUBLANES, s // ts),
        in_specs=[blk, _const_spec((1, d)), _const_spec(w_in.shape), _const_spec(conv_w.shape),
                  _const_spec((1, d)), _const_spec(w_r.shape), _const_spec((1, d)), _const_spec(w_i.shape),
                  _const_spec((1, d)), _const_spec((1, d)), _const_spec(w_out.shape), _const_spec((1, d))],
        out_specs=blk,
        out_shape=jax.ShapeDtypeStruct((b, s, d), F32),
        scratch_shapes=[
            pltpu.VMEM((d // LANES, rows, LANES), F32),
            pltpu.VMEM(((CONV_WIDTH - 1) * SUBLANES, LRU_WIDTH), F32),
            pltpu.VMEM((SUBLANES, LRU_WIDTH), F32),
            pltpu.VMEM((rows, LRU_WIDTH), F32),
            pltpu.VMEM((rows, LRU_WIDTH), F32),
            pltpu.VMEM((d // LANES, rows, LANES), F32),
        ],
        compiler_params=_params("parallel", "arbitrary"),
        name="rglru_block",
    )(x, g_pre, w_in, conv_w, conv_b, w_r, b_r, w_i, b_i, lam, w_out, g_post)


def _tiles(s):
    return dict(even_in_rows=min(512, s), s5_steps=min(256, s), ffn_rows=512, rglru_steps=min(64, s))


def kernel(x, positions, norm_mix_pre, norm_mix_post, norm_ffn_pre, norm_ffn_post, ev_w_in, ev_w_out,
           s5_a_re, s5_a_im, s5_log_dt, s5_b_re, s5_b_im, s5_c_re, s5_c_im, s5_d, s5_w_glu,
           od_w_in, od_w_out, rg_conv_w, rg_conv_b, rg_w_r, rg_b_r, rg_w_i, rg_b_i, rg_lam,
           ffn_w_gate, ffn_w_up, ffn_w_down):
    b, s, d = x.shape
    assert d == D_MODEL and b % SUBLANES == 0 and s % (16 * ATT_BLOCK) == 0
    tiles = _tiles(s)
    row = lambda v: v.astype(F32).reshape(1, -1)
    bf = lambda w: w.astype(BF16)

    q, k, v, u = _even_in(x, positions, row(norm_mix_pre[0]), bf(ev_w_in[0]), tiles["even_in_rows"])
    att = _attention(q, k, v)
    yg = _s5(u, *_s5_params(s5_a_re[0], s5_a_im[0], s5_log_dt[0], s5_b_re[0], s5_b_im[0],
                            s5_c_re[0], s5_c_im[0], s5_d[0]), tiles["s5_steps"])
    x2 = _even_post_ffn(att.reshape(b * s, ATT_WIDTH), yg.reshape(b * s, SSM_WIDTH), x.reshape(b * s, d),
                        bf(s5_w_glu[0]), bf(ev_w_out[0]), row(norm_mix_post[0]),
                        row(norm_ffn_pre[0]), bf(ffn_w_gate[0]), bf(ffn_w_up[0]), bf(ffn_w_down[0]),
                        row(norm_ffn_post[0]), tiles["ffn_rows"])

    x3 = _rglru(x2.reshape(b, s, d), row(norm_mix_pre[1]), bf(od_w_in[0]), rg_conv_w[0].astype(F32),
                row(rg_conv_b[0]), bf(rg_w_r[0]), row(rg_b_r[0]), bf(rg_w_i[0]), row(rg_b_i[0]),
                row(rg_lam[0]), bf(od_w_out[0]), row(norm_mix_post[1]), tiles["rglru_steps"])
    x4 = _ffn(x3.reshape(b * s, d), row(norm_ffn_pre[1]), bf(ffn_w_gate[1]), bf(ffn_w_up[1]),
              bf(ffn_w_down[1]), row(norm_ffn_post[1]), tiles["ffn_rows"])
    return x4.reshape(b, s, d)
```

```python
import functools

import jax
import jax.numpy as jnp
from jax import lax
from jax.experimental import pallas as pl
from jax.experimental.pallas import tpu as pltpu

F32 = jnp.float32
BF16 = jnp.bfloat16

D_MODEL = 1024
HEAD_DIM = 64
ATT_WIDTH = 512
ROPE_DIM = 16
ROPE_THETA = 500000.0
ATT_BLOCK = 128
NEG_INF = -1e30
SSM_WIDTH = 512
SSM_GROUP = 16
SSM_GROUPS = 32
SSM_STATE = 64
LRU_WIDTH = 1024
LRU_BLOCKS = 4
LRU_BLOCK_DIM = 256
CONV_WIDTH = 4
RG_C = 8.0
FFN_HIDDEN = 2816
NORM_EPS = 1e-6

LANES = 128
SUBLANES = 8
N_SLABS = ATT_WIDTH // LANES
GROUPS_PER_SLAB = LANES // SSM_GROUP
SLAB_STATE = GROUPS_PER_SLAB * SSM_STATE
ATT_UNROLL = 16
VMEM_LIMIT_BYTES = 56 * 1024 * 1024


def _rms_norm(x, g):
    return x * lax.rsqrt(jnp.mean(x * x, axis=-1, keepdims=True) + NORM_EPS) * g


def _const_spec(shape):
    n = len(shape)
    return pl.BlockSpec(shape, lambda *_: (0,) * n, pipeline_mode=pl.Buffered(1))


def _params(*sem):
    return pltpu.CompilerParams(dimension_semantics=sem, vmem_limit_bytes=VMEM_LIMIT_BYTES)


def _even_in_kernel(x_ref, pos_ref, g_ref, w_ref, invf_ref, sgn_ref, q_ref, k_ref, v_ref, u_ref):
    hn = _rms_norm(x_ref[0], g_ref[...])
    z = jnp.dot(hn.astype(BF16), w_ref[...], preferred_element_type=F32)
    ang = pos_ref[0].astype(F32) * invf_ref[...]
    cos = jnp.cos(ang)
    sin = jnp.sin(ang) * sgn_ref[...]
    lane = lax.broadcasted_iota(jnp.int32, (1, LANES), 1) % HEAD_DIM
    first_half = lane < (ROPE_DIM // 2)

    def rope(t):
        partner = jnp.where(first_half, pltpu.roll(t, LANES - ROPE_DIM // 2, 1),
                            pltpu.roll(t, ROPE_DIM // 2, 1))
        return t * cos + partner * sin

    for j in range(N_SLABS):
        lo = j * LANES
        q_ref[0, j] = rope(z[:, lo:lo + LANES]) * (HEAD_DIM ** -0.5)
        k_ref[0, j] = rope(z[:, ATT_WIDTH + lo:ATT_WIDTH + lo + LANES])
        v_ref[0, j] = z[:, 2 * ATT_WIDTH + lo:2 * ATT_WIDTH + lo + LANES]
    u_ref[0] = z[:, 3 * ATT_WIDTH:]


def _even_in(x, positions, g, w_in, tm):
    b, s, d = x.shape
    half = ROPE_DIM // 2
    inv_freq = ROPE_THETA ** (-(jnp.arange(half, dtype=F32) * 2.0 / ROPE_DIM))
    lane = jnp.arange(LANES) % HEAD_DIM
    invf = jnp.where(lane < ROPE_DIM, inv_freq[lane % half], 0.0).astype(F32)[None, :]
    sgn = jnp.where(lane < half, -1.0, jnp.where(lane < ROPE_DIM, 1.0, 0.0)).astype(F32)[None, :]
    slab = jax.ShapeDtypeStruct((b, N_SLABS, s, LANES), F32)
    slab_spec = pl.BlockSpec((1, N_SLABS, tm, LANES), lambda i, j: (i, 0, j, 0))
    return pl.pallas_call(
        _even_in_kernel,
        grid=(b, s // tm),
        in_specs=[
            pl.BlockSpec((1, tm, d), lambda i, j: (i, j, 0)),
            pl.BlockSpec((1, tm, 1), lambda i, j: (i, j, 0)),
            _const_spec((1, d)),
            _const_spec(w_in.shape),
            _const_spec((1, LANES)),
            _const_spec((1, LANES)),
        ],
        out_specs=[slab_spec, slab_spec, slab_spec,
                   pl.BlockSpec((1, tm, SSM_WIDTH), lambda i, j: (i, j, 0))],
        out_shape=[slab, slab, slab, jax.ShapeDtypeStruct((b, s, SSM_WIDTH), F32)],
        compiler_params=_params("parallel", "parallel"),
        name="even_in",
    )(x, positions.reshape(b, s, 1), g, w_in, invf, sgn)


def _att_unroll(trips):
    for u in (ATT_UNROLL, 15, 12, 5, 3, 2):
        if trips % u == 0:
            return u
    return 1


def _attn_kernel(q_ref, k_ref, v_ref, o_ref, acc_ref, m_ref, l_ref, *, seq):
    blk = ATT_BLOCK
    head0 = lax.broadcasted_iota(jnp.int32, (1, LANES), 1) < HEAD_DIM

    def bias(nk):
        qi = lax.broadcasted_iota(jnp.int32, (2 * blk, nk), 0) % blk + (nk - blk)
        ki = lax.broadcasted_iota(jnp.int32, (2 * blk, nk), 1)
        dist = qi - ki
        return jnp.where((dist >= 0) & (dist <= blk), 0.0, NEG_INF).astype(F32)

    biases = {blk: bias(blk), 2 * blk: bias(2 * blk)}

    def unit(qb, kb, vb, nk):
        q2 = jnp.concatenate([jnp.where(head0, qb, 0.0), jnp.where(head0, 0.0, qb)], axis=0).astype(BF16)
        s = lax.dot_general(q2, kb.astype(BF16), (((1,), (1,)), ((), ())), preferred_element_type=F32)
        s = s + biases[nk]
        half_max = s if nk == blk else jnp.maximum(s[:, :blk], s[:, blk:])
        m = jnp.max(half_max, axis=-1, keepdims=True)
        p = jnp.exp(s - m).astype(BF16)
        v_ext = jnp.concatenate([vb.astype(BF16), jnp.ones((nk, LANES), BF16)], axis=1)
        o = jnp.dot(p, v_ext, preferred_element_type=F32)
        return (jnp.where(head0, o[:blk, :LANES], o[blk:, :LANES]),
                jnp.where(head0, m[:blk], m[blk:]),
                jnp.where(head0, o[:blk, LANES:], o[blk:, LANES:]))

    def run_branch(br, dil):
        n_blocks = seq // dil // blk

        def rows(start, size):
            if dil == 1:
                return pl.ds(start, size)
            return pl.ds(start, size, stride=dil)

        def store(start, res):
            acc, m, l = res
            acc_ref[br, rows(start, blk), :] = acc
            m_ref[br, rows(start, blk), :] = m
            l_ref[br, rows(start, blk), :] = l

        def first_block(r, carry):
            store(r, unit(q_ref[0, 0, rows(r, blk), :], k_ref[0, 0, rows(r, blk), :],
                          v_ref[0, 0, rows(r, blk), :], blk))
            return carry

        lax.fori_loop(0, dil, first_block, 0, unroll=_att_unroll(dil))

        if n_blocks > 1:
            def later_block(i, carry):
                r = i // (n_blocks - 1)
                n = i % (n_blocks - 1) + 1
                q0 = r + dil * blk * n
                k0 = q0 - dil * blk
                store(q0, unit(q_ref[0, 0, rows(q0, blk), :], k_ref[0, 0, rows(k0, 2 * blk), :],
                               v_ref[0, 0, rows(k0, 2 * blk), :], 2 * blk))
                return carry

            lax.fori_loop(0, dil * (n_blocks - 1), later_block, 0,
                          unroll=_att_unroll(dil * (n_blocks - 1)))

    run_branch(0, 1)
    run_branch(1, 4)
    run_branch(2, 16)

    m0, m1, m2 = m_ref[0], m_ref[1], m_ref[2]
    mx = jnp.maximum(jnp.maximum(m0, m1), m2)
    w0, w1, w2 = jnp.exp(m0 - mx), jnp.exp(m1 - mx), jnp.exp(m2 - mx)
    num = w0 * acc_ref[0] + w1 * acc_ref[1] + w2 * acc_ref[2]
    den = w0 * l_ref[0] + w1 * l_ref[1] + w2 * l_ref[2]
    o_ref[0] = (num / den).astype(o_ref.dtype)


def _attention(q, k, v):
    b, n_slabs, s, _ = q.shape
    spec = pl.BlockSpec((1, 1, s, LANES), lambda i, j: (i, j, 0, 0))
    return pl.pallas_call(
        functools.partial(_attn_kernel, seq=s),
        grid=(b, n_slabs),
        in_specs=[spec, spec, spec],
        out_specs=pl.BlockSpec((1, s, LANES), lambda i, j: (i, 0, j)),
        out_shape=jax.ShapeDtypeStruct((b, s, n_slabs * LANES), BF16),
        scratch_shapes=[pltpu.VMEM((3, s, LANES), F32)] * 3,
        compiler_params=_params("parallel", "parallel"),
        name="dilated_attention",
    )(q, k, v)


def _s5_kernel(u_ref, bm_ref, are_ref, aim_ref, cm_ref, d_ref, o_ref, utb, hs, hcar, ys, *, ts):
    @pl.when(pl.program_id(2) == 0)
    def _():
        hcar[...] = jnp.zeros_like(hcar)

    for b in range(SUBLANES):
        utb[pl.ds(b, ts, stride=SUBLANES), :] = u_ref[b]
    u_tb = utb[...]
    hs[...] = jnp.dot(u_tb.astype(BF16), bm_ref[0], preferred_element_type=F32)
    ar = jnp.broadcast_to(are_ref[0], (SUBLANES, SLAB_STATE))
    ai = jnp.broadcast_to(aim_ref[0], (SUBLANES, SLAB_STATE))

    def step(t, carry):
        hr, hi = carry
        row = pl.multiple_of(t * SUBLANES, SUBLANES)
        xr = hs[pl.ds(row, SUBLANES), :SLAB_STATE]
        xi = hs[pl.ds(row, SUBLANES), SLAB_STATE:]
        nr = ar * hr - ai * hi + xr
        ni = ar * hi + ai * hr + xi
        hs[pl.ds(row, SUBLANES), :SLAB_STATE] = nr
        hs[pl.ds(row, SUBLANES), SLAB_STATE:] = ni
        return nr, ni

    hr, hi = lax.fori_loop(0, ts, step, (hcar[:, :SLAB_STATE], hcar[:, SLAB_STATE:]), unroll=4)
    hcar[:, :SLAB_STATE] = hr
    hcar[:, SLAB_STATE:] = hi

    y = jnp.dot(hs[...].astype(BF16), cm_ref[0], preferred_element_type=F32)
    ys[...] = jax.nn.gelu(y + d_ref[0] * u_tb)
    for b in range(SUBLANES):
        o_ref[b] = ys[pl.ds(b, ts, stride=SUBLANES), :]


def _s5(u, bmat, a_re, a_im, cmat, dskip, ts):
    b, s, w = u.shape
    rows = SUBLANES * ts
    return pl.pallas_call(
        functools.partial(_s5_kernel, ts=ts),
        grid=(b // SUBLANES, N_SLABS, s // ts),
        in_specs=[
            pl.BlockSpec((SUBLANES, ts, LANES), lambda i, j, t: (i, t, j)),
            pl.BlockSpec((1, LANES, 2 * SLAB_STATE), lambda i, j, t: (j, 0, 0)),
            pl.BlockSpec((1, 1, SLAB_STATE), lambda i, j, t: (j, 0, 0)),
            pl.BlockSpec((1, 1, SLAB_STATE), lambda i, j, t: (j, 0, 0)),
            pl.BlockSpec((1, 2 * SLAB_STATE, LANES), lambda i, j, t: (j, 0, 0)),
            pl.BlockSpec((1, 1, LANES), lambda i, j, t: (j, 0, 0)),
        ],
        out_specs=pl.BlockSpec((SUBLANES, ts, LANES), lambda i, j, t: (i, t, j)),
        out_shape=jax.ShapeDtypeStruct((b, s, w), F32),
        scratch_shapes=[
            pltpu.VMEM((rows, LANES), F32),
            pltpu.VMEM((rows, 2 * SLAB_STATE), F32),
            pltpu.VMEM((SUBLANES, 2 * SLAB_STATE), F32),
            pltpu.VMEM((rows, LANES), F32),
        ],
        compiler_params=_params("parallel", "parallel", "arbitrary"),
        name="s5_scan",
    )(u, bmat, a_re, a_im, cmat, dskip)


def _s5_params(a_re, a_im, log_dt, b_re, b_im, c_re, c_im, d_skip):
    a = lax.complex(a_re.astype(F32), a_im.astype(F32))
    dt = jnp.exp(log_dt.astype(F32))[:, None]
    a_bar = jnp.exp(a * dt)
    b_bar = ((a_bar - 1.0) / a)[..., None] * lax.complex(b_re.astype(F32), b_im.astype(F32))
    eye = jnp.eye(GROUPS_PER_SLAB, dtype=F32)

    def block_diag(m):
        g, r, c = m.shape
        m = m.reshape(N_SLABS, GROUPS_PER_SLAB, r, c)
        return jnp.einsum('sgrc,gh->sgrhc', m, eye).reshape(N_SLABS, GROUPS_PER_SLAB * r, GROUPS_PER_SLAB * c)

    b_t = jnp.swapaxes(b_bar, 1, 2)
    bmat = jnp.concatenate([block_diag(jnp.real(b_t)), block_diag(jnp.imag(b_t))], axis=-1)
    c_re_t = jnp.swapaxes(c_re.astype(F32), 1, 2)
    c_im_t = jnp.swapaxes(c_im.astype(F32), 1, 2)
    cmat = jnp.concatenate([block_diag(c_re_t), block_diag(-c_im_t)], axis=1)
    return (bmat.astype(BF16),
            jnp.real(a_bar).reshape(N_SLABS, 1, SLAB_STATE),
            jnp.imag(a_bar).reshape(N_SLABS, 1, SLAB_STATE),
            cmat.astype(BF16),
            d_skip.astype(F32).reshape(N_SLABS, 1, LANES))


def _ffn_block(x, g_pre, w_gate, w_up, w_down, g_post):
    hn = _rms_norm(x, g_pre).astype(BF16)
    gate = jnp.dot(hn, w_gate, preferred_element_type=F32)
    up = jnp.dot(hn, w_up, preferred_element_type=F32)
    act = (jax.nn.silu(gate) * up).astype(BF16)
    ff = jnp.dot(act, w_down, preferred_element_type=F32)
    return x + _rms_norm(ff, g_post)


def _ffn_kernel(x_ref, gpre_ref, wg_ref, wu_ref, wd_ref, gpost_ref, o_ref):
    o_ref[...] = _ffn_block(x_ref[...], gpre_ref[...], wg_ref[...], wu_ref[...], wd_ref[...], gpost_ref[...])


def _ffn(x, g_pre, w_gate, w_up, w_down, g_post, tm):
    t, d = x.shape
    row_spec = pl.BlockSpec((tm, d), lambda i: (i, 0))
    return pl.pallas_call(
        _ffn_kernel,
        grid=(t // tm,),
        in_specs=[row_spec, _const_spec((1, d)), _const_spec(w_gate.shape), _const_spec(w_up.shape),
                  _const_spec(w_down.shape), _const_spec((1, d))],
        out_specs=row_spec,
        out_shape=jax.ShapeDtypeStruct((t, d), F32),
        compiler_params=_params("parallel"),
        name="ffn",
    )(x, g_pre, w_gate, w_up, w_down, g_post)


def _even_post_kernel(att_ref, yg_ref, x_ref, wglu_ref, wout_ref, gpost_ref,
                      gfpre_ref, wg_ref, wu_ref, wd_ref, gfpost_ref, o_ref):
    yg = yg_ref[...]
    glu = jnp.dot(yg.astype(BF16), wglu_ref[...], preferred_element_type=F32)
    ssm = (yg * jax.nn.sigmoid(glu)).astype(BF16)
    mix = (jnp.dot(att_ref[...], wout_ref[:ATT_WIDTH, :], preferred_element_type=F32)
           + jnp.dot(ssm, wout_ref[ATT_WIDTH:, :], preferred_element_type=F32))
    x1 = x_ref[...] + _rms_norm(mix, gpost_ref[...])
    o_ref[...] = _ffn_block(x1, gfpre_ref[...], wg_ref[...], wu_ref[...], wd_ref[...], gfpost_ref[...])


def _even_post_ffn(att, yg, x, w_glu, w_out, g_post, g_fpre, w_gate, w_up, w_down, g_fpost, tm):
    t, d = x.shape
    row = lambda w: pl.BlockSpec((tm, w), lambda i: (i, 0))
    return pl.pallas_call(
        _even_post_kernel,
        grid=(t // tm,),
        in_specs=[row(ATT_WIDTH), row(SSM_WIDTH), row(d), _const_spec(w_glu.shape), _const_spec(w_out.shape),
                  _const_spec((1, d)), _const_spec((1, d)), _const_spec(w_gate.shape),
                  _const_spec(w_up.shape), _const_spec(w_down.shape), _const_spec((1, d))],
        out_specs=row(d),
        out_shape=jax.ShapeDtypeStruct((t, d), F32),
        compiler_params=_params("parallel"),
        name="even_post_ffn",
    )(att, yg, x, w_glu, w_out, g_post, g_fpre, w_gate, w_up, w_down, g_fpost)


def _rglru_kernel(x_ref, gpre_ref, win_ref, convw_ref, convb_ref, wr_ref, br_ref, wi_ref, bi_ref, lam_ref,
                  wout_ref, gpost_ref, o_ref, hn_tb, xprev, hcar, a_s, b_s, mix_tb, *, ts):
    rows = SUBLANES * ts
    n_slabs = LRU_WIDTH // LANES
    tail = (CONV_WIDTH - 1) * SUBLANES

    @pl.when(pl.program_id(1) == 0)
    def _():
        xprev[...] = jnp.zeros_like(xprev)
        hcar[...] = jnp.zeros_like(hcar)

    for b in range(SUBLANES):
        hn = _rms_norm(x_ref[b], gpre_ref[...])
        for j in range(n_slabs):
            hn_tb[j, pl.ds(b, ts, stride=SUBLANES), :] = hn[:, j * LANES:(j + 1) * LANES]
    hn_rows = jnp.concatenate([hn_tb[j] for j in range(n_slabs)], axis=1).astype(BF16)
    z = jnp.dot(hn_rows, win_ref[...], preferred_element_type=F32)
    xb = z[:, :LRU_WIDTH]
    gate_branch = z[:, LRU_WIDTH:]

    xext = jnp.concatenate([xprev[...], xb], axis=0)
    xc = convb_ref[...]
    for kk in range(CONV_WIDTH):
        xc = xc + convw_ref[kk:kk + 1, :] * xext[kk * SUBLANES:kk * SUBLANES + rows]
    xprev[...] = xb[rows - tail:]

    xcb = xc.astype(BF16)

    def block_gates(w_ref, bias_ref):
        parts = [jnp.dot(xcb[:, h * LRU_BLOCK_DIM:(h + 1) * LRU_BLOCK_DIM], w_ref[h],
                         preferred_element_type=F32) for h in range(LRU_BLOCKS)]
        return jax.nn.sigmoid(jnp.concatenate(parts, axis=1) + bias_ref[...])

    gate_r = block_gates(wr_ref, br_ref)
    gate_i = block_gates(wi_ref, bi_ref)
    log_a = -RG_C * gate_r * jax.nn.softplus(-lam_ref[...])
    a = jnp.exp(log_a)
    mult = jnp.sqrt(-jnp.tanh(log_a) * (1.0 + a * a))
    a_s[...] = a
    b_s[...] = mult * (gate_i * xc)

    def step(t, h):
        row = pl.multiple_of(t * SUBLANES, SUBLANES)
        h = a_s[pl.ds(row, SUBLANES), :] * h + b_s[pl.ds(row, SUBLANES), :]
        b_s[pl.ds(row, SUBLANES), :] = h
        return h

    hcar[...] = lax.fori_loop(0, ts, step, hcar[...], unroll=4)

    y = (b_s[...] * jax.nn.gelu(gate_branch)).astype(BF16)
    mix = _rms_norm(jnp.dot(y, wout_ref[...], preferred_element_type=F32), gpost_ref[...])
    for j in range(n_slabs):
        mix_tb[j] = mix[:, j * LANES:(j + 1) * LANES]
    for b in range(SUBLANES):
        mix_b = jnp.concatenate([mix_tb[j, pl.ds(b, ts, stride=SUBLANES), :] for j in range(n_slabs)], axis=1)
        o_ref[b] = x_ref[b] + mix_b


def _rglru(x, g_pre, w_in, conv_w, conv_b, w_r, b_r, w_i, b_i, lam, w_out, g_post, ts):
    b, s, d = x.shape
    rows = SUBLANES * ts
    blk = pl.BlockSpec((SUBLANES, ts, d), lambda i, t: (i, t, 0))
    return pl.pallas_call(
        functools.partial(_rglru_kernel, ts=ts),
        grid=(b // SUBLANES, s // ts),
        in_specs=[blk, _const_spec((1, d)), _const_spec(w_in.shape), _const_spec(conv_w.shape),
                  _const_spec((1, d)), _const_spec(w_r.shape), _const_spec((1, d)), _const_spec(w_i.shape),
                  _const_spec((1, d)), _const_spec((1, d)), _const_spec(w_out.shape), _const_spec((1, d))],
        out_specs=blk,
        out_shape=jax.ShapeDtypeStruct((b, s, d), F32),
        scratch_shapes=[
            pltpu.VMEM((d // LANES, rows, LANES), F32),
            pltpu.VMEM(((CONV_WIDTH - 1) * SUBLANES, LRU_WIDTH), F32),
            pltpu.VMEM((SUBLANES, LRU_WIDTH), F32),
            pltpu.VMEM((rows, LRU_WIDTH), F32),
            pltpu.VMEM((rows, LRU_WIDTH), F32),
            pltpu.VMEM((d // LANES, rows, LANES), F32),
        ],
        compiler_params=_params("parallel", "arbitrary"),
        name="rglru_block",
    )(x, g_pre, w_in, conv_w, conv_b, w_r, b_r, w_i, b_i, lam, w_out, g_post)


def _tiles(s):
    return dict(even_in_rows=min(512, s), s5_steps=min(256, s), ffn_rows=512, rglru_steps=min(64, s))


def kernel(x, positions, norm_mix_pre, norm_mix_post, norm_ffn_pre, norm_ffn_post, ev_w_in, ev_w_out,
           s5_a_re, s5_a_im, s5_log_dt, s5_b_re, s5_b_im, s5_c_re, s5_c_im, s5_d, s5_w_glu,
           od_w_in, od_w_out, rg_conv_w, rg_conv_b, rg_w_r, rg_b_r, rg_w_i, rg_b_i, rg_lam,
           ffn_w_gate, ffn_w_up, ffn_w_down):
    b, s, d = x.shape
    assert d == D_MODEL and b % SUBLANES == 0 and s % (16 * ATT_BLOCK) == 0
    tiles = _tiles(s)
    row = lambda v: v.astype(F32).reshape(1, -1)
    bf = lambda w: w.astype(BF16)

    q, k, v, u = _even_in(x, positions, row(norm_mix_pre[0]), bf(ev_w_in[0]), tiles["even_in_rows"])
    att = _attention(q, k, v)
    yg = _s5(u, *_s5_params(s5_a_re[0], s5_a_im[0], s5_log_dt[0], s5_b_re[0], s5_b_im[0],
                            s5_c_re[0], s5_c_im[0], s5_d[0]), tiles["s5_steps"])
    x2 = _even_post_ffn(att.reshape(b * s, ATT_WIDTH), yg.reshape(b * s, SSM_WIDTH), x.reshape(b * s, d),
                        bf(s5_w_glu[0]), bf(ev_w_out[0]), row(norm_mix_post[0]),
                        row(norm_ffn_pre[0]), bf(ffn_w_gate[0]), bf(ffn_w_up[0]), bf(ffn_w_down[0]),
                        row(norm_ffn_post[0]), tiles["ffn_rows"])

    x3 = _rglru(x2.reshape(b, s, d), row(norm_mix_pre[1]), bf(od_w_in[0]), rg_conv_w[0].astype(F32),
                row(rg_conv_b[0]), bf(rg_w_r[0]), row(rg_b_r[0]), bf(rg_w_i[0]), row(rg_b_i[0]),
                row(rg_lam[0]), bf(od_w_out[0]), row(norm_mix_post[1]), tiles["rglru_steps"])
    x4 = _ffn(x3.reshape(b * s, d), row(norm_ffn_pre[1]), bf(ffn_w_gate[1]), bf(ffn_w_up[1]),
              bf(ffn_w_down[1]), row(norm_ffn_post[1]), tiles["ffn_rows"])
    return x4.reshape(b, s, d)
```

```python
import functools

import jax
import jax.numpy as jnp
from jax import lax
from jax.experimental import pallas as pl
from jax.experimental.pallas import tpu as pltpu

F32 = jnp.float32
BF16 = jnp.bfloat16

D_MODEL = 1024
HEAD_DIM = 64
ATT_WIDTH = 512
ROPE_DIM = 16
ROPE_THETA = 500000.0
ATT_BLOCK = 128
NEG_INF = -1e30
SSM_WIDTH = 512
SSM_GROUP = 16
SSM_GROUPS = 32
SSM_STATE = 64
LRU_WIDTH = 1024
LRU_BLOCKS = 4
LRU_BLOCK_DIM = 256
CONV_WIDTH = 4
RG_C = 8.0
FFN_HIDDEN = 2816
NORM_EPS = 1e-6

LANES = 128
SUBLANES = 8
N_SLABS = ATT_WIDTH // LANES
GROUPS_PER_SLAB = LANES // SSM_GROUP
SLAB_STATE = GROUPS_PER_SLAB * SSM_STATE
ATT_UNROLL = 16
VMEM_LIMIT_BYTES = 56 * 1024 * 1024


def _rms_norm(x, g):
    return x * lax.rsqrt(jnp.mean(x * x, axis=-1, keepdims=True) + NORM_EPS) * g


def _const_spec(shape):
    n = len(shape)
    return pl.BlockSpec(shape, lambda *_: (0,) * n, pipeline_mode=pl.Buffered(1))


def _params(*sem):
    return pltpu.CompilerParams(dimension_semantics=sem, vmem_limit_bytes=VMEM_LIMIT_BYTES)


def _even_in_kernel(x_ref, pos_ref, g_ref, w_ref, invf_ref, sgn_ref, q_ref, k_ref, v_ref, u_ref):
    ang = pos_ref[0].astype(F32) * invf_ref[...]
    cos = jnp.cos(ang)
    sin = jnp.sin(ang) * sgn_ref[...]
    hn = _rms_norm(x_ref[0], g_ref[...])
    z = jnp.dot(hn.astype(BF16), w_ref[...], preferred_element_type=F32)
    lane = lax.broadcasted_iota(jnp.int32, (1, LANES), 1) % HEAD_DIM
    first_half = lane < (ROPE_DIM // 2)

    def rope(t):
        partner = jnp.where(first_half, pltpu.roll(t, LANES - ROPE_DIM // 2, 1),
                            pltpu.roll(t, ROPE_DIM // 2, 1))
        return t * cos + partner * sin

    for j in range(N_SLABS):
        lo = j * LANES
        q_ref[0, j] = rope(z[:, lo:lo + LANES]) * (HEAD_DIM ** -0.5)
        k_ref[0, j] = rope(z[:, ATT_WIDTH + lo:ATT_WIDTH + lo + LANES])
        v_ref[0, j] = z[:, 2 * ATT_WIDTH + lo:2 * ATT_WIDTH + lo + LANES]
    u_ref[0] = z[:, 3 * ATT_WIDTH:]


def _even_in(x, positions, g, w_in, tm):
    b, s, d = x.shape
    half = ROPE_DIM // 2
    inv_freq = ROPE_THETA ** (-(jnp.arange(half, dtype=F32) * 2.0 / ROPE_DIM))
    lane = jnp.arange(LANES) % HEAD_DIM
    invf = jnp.where(lane < ROPE_DIM, inv_freq[lane % half], 0.0).astype(F32)[None, :]
    sgn = jnp.where(lane < half, -1.0, jnp.where(lane < ROPE_DIM, 1.0, 0.0)).astype(F32)[None, :]
    slab = jax.ShapeDtypeStruct((b, N_SLABS, s, LANES), F32)
    slab_spec = pl.BlockSpec((1, N_SLABS, tm, LANES), lambda i, j: (i, 0, j, 0))
    return pl.pallas_call(
        _even_in_kernel,
        grid=(b, s // tm),
        in_specs=[
            pl.BlockSpec((1, tm, d), lambda i, j: (i, j, 0)),
            pl.BlockSpec((1, tm, 1), lambda i, j: (i, j, 0)),
            _const_spec((1, d)),
            _const_spec(w_in.shape),
            _const_spec((1, LANES)),
            _const_spec((1, LANES)),
        ],
        out_specs=[slab_spec, slab_spec, slab_spec,
                   pl.BlockSpec((1, tm, SSM_WIDTH), lambda i, j: (i, j, 0))],
        out_shape=[slab, slab, slab, jax.ShapeDtypeStruct((b, s, SSM_WIDTH), F32)],
        compiler_params=_params("parallel", "parallel"),
        name="even_in",
    )(x, positions.reshape(b, s, 1), g, w_in, invf, sgn)


def _att_unroll(trips):
    for u in (ATT_UNROLL, 15, 12, 4):
        if trips % u == 0:
            return u
    return 1


def _attn_kernel(q_ref, k_ref, v_ref, o_ref, acc_ref, m_ref, l_ref, *, seq):
    blk = ATT_BLOCK
    head0 = lax.broadcasted_iota(jnp.int32, (1, LANES), 1) < HEAD_DIM

    def bias(nk):
        qi = lax.broadcasted_iota(jnp.int32, (2 * blk, nk), 0) % blk + (nk - blk)
        ki = lax.broadcasted_iota(jnp.int32, (2 * blk, nk), 1)
        dist = qi - ki
        return jnp.where((dist >= 0) & (dist <= blk), 0.0, NEG_INF).astype(F32)

    biases = {blk: bias(blk), 2 * blk: bias(2 * blk)}

    def unit(qb, kb, vb, nk):
        q2 = jnp.concatenate([jnp.where(head0, qb, 0.0), jnp.where(head0, 0.0, qb)], axis=0).astype(BF16)
        s = lax.dot_general(q2, kb.astype(BF16), (((1,), (1,)), ((), ())), preferred_element_type=F32)
        s = s + biases[nk]
        half_max = s if nk == blk else jnp.maximum(s[:, :blk], s[:, blk:])
        m = jnp.max(half_max, axis=-1, keepdims=True)
        p = jnp.exp(s - m).astype(BF16)
        v_ext = jnp.concatenate([vb.astype(BF16), jnp.ones((nk, LANES), BF16)], axis=1)
        o = jnp.dot(p, v_ext, preferred_element_type=F32)
        return (jnp.where(head0, o[:blk, :LANES], o[blk:, :LANES]),
                jnp.where(head0, m[:blk], m[blk:]),
                jnp.where(head0, o[:blk, LANES:], o[blk:, LANES:]))

    def run_branch(br, dil):
        n_blocks = seq // dil // blk

        def rows(start, size):
            if dil == 1:
                return pl.ds(start, size)
            return pl.ds(start, size, stride=dil)

        def store(start, res):
            acc, m, l = res
            acc_ref[br, rows(start, blk), :] = acc
            m_ref[br, rows(start, blk), :] = m
            l_ref[br, rows(start, blk), :] = l

        def first_block(r, carry):
            store(r, unit(q_ref[0, 0, rows(r, blk), :], k_ref[0, 0, rows(r, blk), :],
                          v_ref[0, 0, rows(r, blk), :], blk))
            return carry

        lax.fori_loop(0, dil, first_block, 0, unroll=_att_unroll(dil))

        if n_blocks > 1:
            def later_block(i, carry):
                r = i // (n_blocks - 1)
                n = i % (n_blocks - 1) + 1
                q0 = r + dil * blk * n
                k0 = q0 - dil * blk
                store(q0, unit(q_ref[0, 0, rows(q0, blk), :], k_ref[0, 0, rows(k0, 2 * blk), :],
                               v_ref[0, 0, rows(k0, 2 * blk), :], 2 * blk))
                return carry

            lax.fori_loop(0, dil * (n_blocks - 1), later_block, 0,
                          unroll=_att_unroll(dil * (n_blocks - 1)))

    run_branch(0, 1)
    run_branch(1, 4)
    run_branch(2, 16)

    m0, m1, m2 = m_ref[0], m_ref[1], m_ref[2]
    mx = jnp.maximum(jnp.maximum(m0, m1), m2)
    w0, w1, w2 = jnp.exp(m0 - mx), jnp.exp(m1 - mx), jnp.exp(m2 - mx)
    num = w0 * acc_ref[0] + w1 * acc_ref[1] + w2 * acc_ref[2]
    den = w0 * l_ref[0] + w1 * l_ref[1] + w2 * l_ref[2]
    o_ref[0] = (num / den).astype(o_ref.dtype)


def _attention(q, k, v):
    b, n_slabs, s, _ = q.shape
    spec = pl.BlockSpec((1, 1, s, LANES), lambda i, j: (i, j, 0, 0))
    return pl.pallas_call(
        functools.partial(_attn_kernel, seq=s),
        grid=(b, n_slabs),
        in_specs=[spec, spec, spec],
        out_specs=pl.BlockSpec((1, s, LANES), lambda i, j: (i, 0, j)),
        out_shape=jax.ShapeDtypeStruct((b, s, n_slabs * LANES), BF16),
        scratch_shapes=[pltpu.VMEM((3, s, LANES), F32)] * 3,
        compiler_params=_params("parallel", "parallel"),
        name="dilated_attention",
    )(q, k, v)


def _s5_kernel(u_ref, bm_ref, are_ref, aim_ref, cm_ref, d_ref, o_ref, utb, hs, hcar, ys, *, ts):
    @pl.when(pl.program_id(2) == 0)
    def _():
        hcar[...] = jnp.zeros_like(hcar)

    for b in range(SUBLANES):
        utb[pl.ds(b, ts, stride=SUBLANES), :] = u_ref[b]
    u_tb = utb[...]
    hs[...] = jnp.dot(u_tb.astype(BF16), bm_ref[0], preferred_element_type=F32)
    ar = jnp.broadcast_to(are_ref[0], (SUBLANES, SLAB_STATE))
    ai = jnp.broadcast_to(aim_ref[0], (SUBLANES, SLAB_STATE))

    def step(t, carry):
        hr, hi = carry
        row = pl.multiple_of(t * SUBLANES, SUBLANES)
        xr = hs[pl.ds(row, SUBLANES), :SLAB_STATE]
        xi = hs[pl.ds(row, SUBLANES), SLAB_STATE:]
        nr = ar * hr - ai * hi + xr
        ni = ar * hi + ai * hr + xi
        hs[pl.ds(row, SUBLANES), :SLAB_STATE] = nr
        hs[pl.ds(row, SUBLANES), SLAB_STATE:] = ni
        return nr, ni

    hr, hi = lax.fori_loop(0, ts, step, (hcar[:, :SLAB_STATE], hcar[:, SLAB_STATE:]), unroll=True)
    hcar[:, :SLAB_STATE] = hr
    hcar[:, SLAB_STATE:] = hi

    y = jnp.dot(hs[...].astype(BF16), cm_ref[0], preferred_element_type=F32)
    ys[...] = jax.nn.gelu(y + d_ref[0] * u_tb)
    for b in range(SUBLANES):
        o_ref[b] = ys[pl.ds(b, ts, stride=SUBLANES), :]


def _s5(u, bmat, a_re, a_im, cmat, dskip, ts):
    b, s, w = u.shape
    rows = SUBLANES * ts
    return pl.pallas_call(
        functools.partial(_s5_kernel, ts=ts),
        grid=(b // SUBLANES, N_SLABS, s // ts),
        in_specs=[
            pl.BlockSpec((SUBLANES, ts, LANES), lambda i, j, t: (i, t, j)),
            pl.BlockSpec((1, LANES, 2 * SLAB_STATE), lambda i, j, t: (j, 0, 0)),
            pl.BlockSpec((1, 1, SLAB_STATE), lambda i, j, t: (j, 0, 0)),
            pl.BlockSpec((1, 1, SLAB_STATE), lambda i, j, t: (j, 0, 0)),
            pl.BlockSpec((1, 2 * SLAB_STATE, LANES), lambda i, j, t: (j, 0, 0)),
            pl.BlockSpec((1, 1, LANES), lambda i, j, t: (j, 0, 0)),
        ],
        out_specs=pl.BlockSpec((SUBLANES, ts, LANES), lambda i, j, t: (i, t, j)),
        out_shape=jax.ShapeDtypeStruct((b, s, w), F32),
        scratch_shapes=[
            pltpu.VMEM((rows, LANES), F32),
            pltpu.VMEM((rows, 2 * SLAB_STATE), F32),
            pltpu.VMEM((SUBLANES, 2 * SLAB_STATE), F32),
            pltpu.VMEM((rows, LANES), F32),
        ],
        compiler_params=_params("parallel", "parallel", "arbitrary"),
        name="s5_scan",
    )(u, bmat, a_re, a_im, cmat, dskip)


def _s5_params(a_re, a_im, log_dt, b_re, b_im, c_re, c_im, d_skip):
    a = lax.complex(a_re.astype(F32), a_im.astype(F32))
    dt = jnp.exp(log_dt.astype(F32))[:, None]
    a_bar = jnp.exp(a * dt)
    b_bar = ((a_bar - 1.0) / a)[..., None] * lax.complex(b_re.astype(F32), b_im.astype(F32))
    eye = jnp.eye(GROUPS_PER_SLAB, dtype=F32)

    def block_diag(m):
        g, r, c = m.shape
        m = m.reshape(N_SLABS, GROUPS_PER_SLAB, r, c)
        return jnp.einsum('sgrc,gh->sgrhc', m, eye).reshape(N_SLABS, GROUPS_PER_SLAB * r, GROUPS_PER_SLAB * c)

    b_t = jnp.swapaxes(b_bar, 1, 2)
    bmat = jnp.concatenate([block_diag(jnp.real(b_t)), block_diag(jnp.imag(b_t))], axis=-1)
    c_re_t = jnp.swapaxes(c_re.astype(F32), 1, 2)
    c_im_t = jnp.swapaxes(c_im.astype(F32), 1, 2)
    cmat = jnp.concatenate([block_diag(c_re_t), block_diag(-c_im_t)], axis=1)
    return (bmat.astype(BF16),
            jnp.real(a_bar).reshape(N_SLABS, 1, SLAB_STATE),
            jnp.imag(a_bar).reshape(N_SLABS, 1, SLAB_STATE),
            cmat.astype(BF16),
            d_skip.astype(F32).reshape(N_SLABS, 1, LANES))


def _ffn_block(x, g_pre, w_gate, w_up, w_down, g_post):
    hn = _rms_norm(x, g_pre).astype(BF16)
    gate = jnp.dot(hn, w_gate, preferred_element_type=F32)
    up = jnp.dot(hn, w_up, preferred_element_type=F32)
    act = (jax.nn.silu(gate) * up).astype(BF16)
    ff = jnp.dot(act, w_down, preferred_element_type=F32)
    return x + _rms_norm(ff, g_post)


def _ffn_kernel(x_ref, gpre_ref, wg_ref, wu_ref, wd_ref, gpost_ref, o_ref):
    o_ref[...] = _ffn_block(x_ref[...], gpre_ref[...], wg_ref[...], wu_ref[...], wd_ref[...], gpost_ref[...])


def _ffn(x, g_pre, w_gate, w_up, w_down, g_post, tm):
    t, d = x.shape
    row_spec = pl.BlockSpec((tm, d), lambda i: (i, 0))
    return pl.pallas_call(
        _ffn_kernel,
        grid=(t // tm,),
        in_specs=[row_spec, _const_spec((1, d)), _const_spec(w_gate.shape), _const_spec(w_up.shape),
                  _const_spec(w_down.shape), _const_spec((1, d))],
        out_specs=row_spec,
        out_shape=jax.ShapeDtypeStruct((t, d), F32),
        compiler_params=_params("parallel"),
        name="ffn",
    )(x, g_pre, w_gate, w_up, w_down, g_post)


def _even_post_kernel(att_ref, yg_ref, x_ref, wglu_ref, wout_ref, gpost_ref,
                      gfpre_ref, wg_ref, wu_ref, wd_ref, gfpost_ref, o_ref):
    yg = yg_ref[...]
    glu = jnp.dot(yg.astype(BF16), wglu_ref[...], preferred_element_type=F32)
    ssm = (yg * jax.nn.sigmoid(glu)).astype(BF16)
    mix = (jnp.dot(att_ref[...], wout_ref[:ATT_WIDTH, :], preferred_element_type=F32)
           + jnp.dot(ssm, wout_ref[ATT_WIDTH:, :], preferred_element_type=F32))
    x1 = x_ref[...] + _rms_norm(mix, gpost_ref[...])
    o_ref[...] = _ffn_block(x1, gfpre_ref[...], wg_ref[...], wu_ref[...], wd_ref[...], gfpost_ref[...])


def _even_post_ffn(att, yg, x, w_glu, w_out, g_post, g_fpre, w_gate, w_up, w_down, g_fpost, tm):
    t, d = x.shape
    row = lambda w: pl.BlockSpec((tm, w), lambda i: (i, 0))
    return pl.pallas_call(
        _even_post_kernel,
        grid=(t // tm,),
        in_specs=[row(ATT_WIDTH), row(SSM_WIDTH), row(d), _const_spec(w_glu.shape), _const_spec(w_out.shape),
                  _const_spec((1, d)), _const_spec((1, d)), _const_spec(w_gate.shape),
                  _const_spec(w_up.shape), _const_spec(w_down.shape), _const_spec((1, d))],
        out_specs=row(d),
        out_shape=jax.ShapeDtypeStruct((t, d), F32),
        compiler_params=_params("parallel"),
        name="even_post_ffn",
    )(att, yg, x, w_glu, w_out, g_post, g_fpre, w_gate, w_up, w_down, g_fpost)


def _rglru_kernel(x_ref, gpre_ref, win_ref, convw_ref, convb_ref, wr_ref, br_ref, wi_ref, bi_ref, lam_ref,
                  wout_ref, gpost_ref, o_ref, hn_tb, xprev, hcar, a_s, b_s, mix_tb, *, ts):
    rows = SUBLANES * ts
    n_slabs = LRU_WIDTH // LANES
    tail = (CONV_WIDTH - 1) * SUBLANES

    @pl.when(pl.program_id(1) == 0)
    def _():
        xprev[...] = jnp.zeros_like(xprev)
        hcar[...] = jnp.zeros_like(hcar)

    for b in range(SUBLANES):
        hn = _rms_norm(x_ref[b], gpre_ref[...])
        for j in range(n_slabs):
            hn_tb[j, pl.ds(b, ts, stride=SUBLANES), :] = hn[:, j * LANES:(j + 1) * LANES]
    hn_rows = jnp.concatenate([hn_tb[j] for j in range(n_slabs)], axis=1).astype(BF16)
    z = jnp.dot(hn_rows, win_ref[...], preferred_element_type=F32)
    xb = z[:, :LRU_WIDTH]
    gate_branch = z[:, LRU_WIDTH:]

    xext = jnp.concatenate([xprev[...], xb], axis=0)
    xc = convb_ref[...]
    for kk in range(CONV_WIDTH):
        xc = xc + convw_ref[kk:kk + 1, :] * xext[kk * SUBLANES:kk * SUBLANES + rows]
    xprev[...] = xb[rows - tail:]

    xcb = xc.astype(BF16)

    def block_gates(w_ref, bias_ref):
        parts = [jnp.dot(xcb[:, h * LRU_BLOCK_DIM:(h + 1) * LRU_BLOCK_DIM], w_ref[h],
                         preferred_element_type=F32) for h in range(LRU_BLOCKS)]
        return jax.nn.sigmoid(jnp.concatenate(parts, axis=1) + bias_ref[...])

    gate_r = block_gates(wr_ref, br_ref)
    gate_i = block_gates(wi_ref, bi_ref)
    log_a = gate_r * (-RG_C * jax.nn.softplus(-lam_ref[...]))
    a = jnp.exp(log_a)
    v = -jnp.tanh(log_a) * (1.0 + a * a)
    mult = jnp.where(v > 0.0, v * lax.rsqrt(v), 0.0)
    a_s[...] = a
    b_s[...] = mult * (gate_i * xc)

    def step(t, h):
        row = pl.multiple_of(t * SUBLANES, SUBLANES)
        h = a_s[pl.ds(row, SUBLANES), :] * h + b_s[pl.ds(row, SUBLANES), :]
        b_s[pl.ds(row, SUBLANES), :] = h
        return h

    hcar[...] = lax.fori_loop(0, ts, step, hcar[...], unroll=True)

    y = (b_s[...] * jax.nn.gelu(gate_branch)).astype(BF16)
    mix = _rms_norm(jnp.dot(y, wout_ref[...], preferred_element_type=F32), gpost_ref[...])
    for j in range(n_slabs):
        mix_tb[j] = mix[:, j * LANES:(j + 1) * LANES]
    for b in range(SUBLANES):
        mix_b = jnp.concatenate([mix_tb[j, pl.ds(b, ts, stride=SUBLANES), :] for j in range(n_slabs)], axis=1)
        o_ref[b] = x_ref[b] + mix_b


def _rglru(x, g_pre, w_in, conv_w, conv_b, w_r, b_r, w_i, b_i, lam, w_out, g_post, ts):
    b, s, d = x.shape
    rows = SUBLANES * ts
    blk = pl.BlockSpec((SUBLANES, ts, d), lambda i, t: (i, t, 0))
    return pl.pallas_call(
        functools.partial(_rglru_kernel, ts=ts),
        grid=(b // SUBLANES, s // ts),
        in_specs=[blk, _const_spec((1, d)), _const_spec(w_in.shape), _const_spec(conv_w.shape),
                  _const_spec((1, d)), _const_spec(w_r.shape), _const_spec((1, d)), _const_spec(w_i.shape),
                  _const_spec((1, d)), _const_spec((1, d)), _const_spec(w_out.shape), _const_spec((1, d))],
        out_specs=blk,
        out_shape=jax.ShapeDtypeStruct((b, s, d), F32),
        scratch_shapes=[
            pltpu.VMEM((d // LANES, rows, LANES), F32),
            pltpu.VMEM(((CONV_WIDTH - 1) * SUBLANES, LRU_WIDTH), F32),
            pltpu.VMEM((SUBLANES, LRU_WIDTH), F32),
            pltpu.VMEM((rows, LRU_WIDTH), F32),
            pltpu.VMEM((rows, LRU_WIDTH), F32),
            pltpu.VMEM((d // LANES, rows, LANES), F32),
        ],
        compiler_params=_params("parallel", "arbitrary"),
        name="rglru_block",
    )(x, g_pre, w_in, conv_w, conv_b, w_r, b_r, w_i, b_i, lam, w_out, g_post)


def _tiles(s):
    return dict(even_in_rows=min(512, s), s5_steps=min(256, s), ffn_rows=512, rglru_steps=min(64, s))


def kernel(x, positions, norm_mix_pre, norm_mix_post, norm_ffn_pre, norm_ffn_post, ev_w_in, ev_w_out,
           s5_a_re, s5_a_im, s5_log_dt, s5_b_re, s5_b_im, s5_c_re, s5_c_im, s5_d, s5_w_glu,
           od_w_in, od_w_out, rg_conv_w, rg_conv_b, rg_w_r, rg_b_r, rg_w_i, rg_b_i, rg_lam,
           ffn_w_gate, ffn_w_up, ffn_w_down):
    b, s, d = x.shape
    assert d == D_MODEL and b % SUBLANES == 0 and s % (16 * ATT_BLOCK) == 0
    tiles = _tiles(s)
    row = lambda v: v.astype(F32).reshape(1, -1)
    bf = lambda w: w.astype(BF16)

    q, k, v, u = _even_in(x, positions, row(norm_mix_pre[0]), bf(ev_w_in[0]), tiles["even_in_rows"])
    att = _attention(q, k, v)
    yg = _s5(u, *_s5_params(s5_a_re[0], s5_a_im[0], s5_log_dt[0], s5_b_re[0], s5_b_im[0],
                            s5_c_re[0], s5_c_im[0], s5_d[0]), tiles["s5_steps"])
    x2 = _even_post_ffn(att.reshape(b * s, ATT_WIDTH), yg.reshape(b * s, SSM_WIDTH), x.reshape(b * s, d),
                        bf(s5_w_glu[0]), bf(ev_w_out[0]), row(norm_mix_post[0]),
                        row(norm_ffn_pre[0]), bf(ffn_w_gate[0]), bf(ffn_w_up[0]), bf(ffn_w_down[0]),
                        row(norm_ffn_post[0]), tiles["ffn_rows"])

    x3 = _rglru(x2.reshape(b, s, d), row(norm_mix_pre[1]), bf(od_w_in[0]), rg_conv_w[0].astype(F32),
                row(rg_conv_b[0]), bf(rg_w_r[0]), row(rg_b_r[0]), bf(rg_w_i[0]), row(rg_b_i[0]),
                row(rg_lam[0]), bf(od_w_out[0]), row(norm_mix_post[1]), tiles["rglru_steps"])
    x4 = _ffn(x3.reshape(b * s, d), row(norm_ffn_pre[1]), bf(ffn_w_gate[1]), bf(ffn_w_up[1]),
              bf(ffn_w_down[1]), row(norm_ffn_post[1]), tiles["ffn_rows"])
    return x4.reshape(b, s, d)
```

```python
import functools
import math

import jax
import jax.numpy as jnp
from jax import lax
from jax.experimental import pallas as pl
from jax.experimental.pallas import tpu as pltpu

F32 = jnp.float32
BF16 = jnp.bfloat16

D_MODEL = 1024
HEAD_DIM = 64
ATT_WIDTH = 512
ROPE_DIM = 16
ROPE_THETA = 500000.0
ATT_BLOCK = 128
NEG_INF = -1e30
SSM_WIDTH = 512
SSM_GROUP = 16
SSM_GROUPS = 32
SSM_STATE = 64
LRU_WIDTH = 1024
LRU_BLOCKS = 4
LRU_BLOCK_DIM = 256
CONV_WIDTH = 4
RG_C = 8.0
FFN_HIDDEN = 2816
NORM_EPS = 1e-6

LANES = 128
SUBLANES = 8
N_SLABS = ATT_WIDTH // LANES
GROUPS_PER_SLAB = LANES // SSM_GROUP
SLAB_STATE = GROUPS_PER_SLAB * SSM_STATE
Q_SCALE = HEAD_DIM ** -0.5 * math.log2(math.e)
VMEM_LIMIT_BYTES = 56 * 1024 * 1024


def _rms_norm(x, g):
    return x * lax.rsqrt(jnp.mean(x * x, axis=-1, keepdims=True) + NORM_EPS) * g


def _const_spec(shape):
    n = len(shape)
    return pl.BlockSpec(shape, lambda *_: (0,) * n, pipeline_mode=pl.Buffered(1))


def _params(*sem):
    return pltpu.CompilerParams(dimension_semantics=sem, vmem_limit_bytes=VMEM_LIMIT_BYTES)


def _even_in_kernel(x_ref, pos_ref, g_ref, w_ref, invf_ref, sgn_ref, q_ref, k_ref, v_ref, u_ref):
    ang = pos_ref[0].astype(F32) * invf_ref[...]
    cos = jnp.cos(ang)
    sin = jnp.sin(ang) * sgn_ref[...]
    hn = _rms_norm(x_ref[0], g_ref[...])
    z = jnp.dot(hn.astype(BF16), w_ref[...], preferred_element_type=F32)
    lane = lax.broadcasted_iota(jnp.int32, (1, LANES), 1) % HEAD_DIM
    first_half = lane < (ROPE_DIM // 2)

    def rope(t):
        partner = jnp.where(first_half, pltpu.roll(t, LANES - ROPE_DIM // 2, 1),
                            pltpu.roll(t, ROPE_DIM // 2, 1))
        return t * cos + partner * sin

    for j in range(N_SLABS):
        lo = j * LANES
        q_ref[0, j] = rope(z[:, lo:lo + LANES]) * Q_SCALE
        k_ref[0, j] = rope(z[:, ATT_WIDTH + lo:ATT_WIDTH + lo + LANES])
        v_ref[0, j] = z[:, 2 * ATT_WIDTH + lo:2 * ATT_WIDTH + lo + LANES]
    u_ref[0] = z[:, 3 * ATT_WIDTH:]


def _even_in(x, positions, g, w_in, tm):
    b, s, d = x.shape
    half = ROPE_DIM // 2
    inv_freq = ROPE_THETA ** (-(jnp.arange(half, dtype=F32) * 2.0 / ROPE_DIM))
    lane = jnp.arange(LANES) % HEAD_DIM
    invf = jnp.where(lane < ROPE_DIM, inv_freq[lane % half], 0.0).astype(F32)[None, :]
    sgn = jnp.where(lane < half, -1.0, jnp.where(lane < ROPE_DIM, 1.0, 0.0)).astype(F32)[None, :]
    slab = jax.ShapeDtypeStruct((b, N_SLABS, s, LANES), F32)
    slab_spec = pl.BlockSpec((1, N_SLABS, tm, LANES), lambda i, j: (i, 0, j, 0))
    return pl.pallas_call(
        _even_in_kernel,
        grid=(b, s // tm),
        in_specs=[
            pl.BlockSpec((1, tm, d), lambda i, j: (i, j, 0)),
            pl.BlockSpec((1, tm, 1), lambda i, j: (i, j, 0)),
            _const_spec((1, d)),
            _const_spec(w_in.shape),
            _const_spec((1, LANES)),
            _const_spec((1, LANES)),
        ],
        out_specs=[slab_spec, slab_spec, slab_spec,
                   pl.BlockSpec((1, tm, SSM_WIDTH), lambda i, j: (i, j, 0))],
        out_shape=[slab, slab, slab, jax.ShapeDtypeStruct((b, s, SSM_WIDTH), F32)],
        compiler_params=_params("parallel", "parallel"),
        name="even_in",
    )(x, positions.reshape(b, s, 1), g, w_in, invf, sgn)


def _attn_kernel(q_ref, k_ref, v_ref, o_ref, q4, k4, v4, on_ref, lse_ref, *, seq):
    blk = ATT_BLOCK
    sub = seq // 4
    head0 = lax.broadcasted_iota(jnp.int32, (1, LANES), 1) < HEAD_DIM

    def bias(nk):
        qi = lax.broadcasted_iota(jnp.int32, (2 * blk, nk), 0) % blk + (nk - blk)
        ki = lax.broadcasted_iota(jnp.int32, (2 * blk, nk), 1)
        dist = qi - ki
        return jnp.where((dist >= 0) & (dist <= blk), 0.0, NEG_INF).astype(F32)

    biases = {blk: bias(blk), 2 * blk: bias(2 * blk)}

    def unit(qb, kb, vb):
        nk = kb.shape[0]
        q2 = jnp.concatenate([jnp.where(head0, qb, 0.0), jnp.where(head0, 0.0, qb)], axis=0).astype(BF16)
        s = lax.dot_general(q2, kb.astype(BF16), (((1,), (1,)), ((), ())), preferred_element_type=F32)
        s = s + biases[nk]
        half_max = s if nk == blk else jnp.maximum(s[:, :blk], s[:, blk:])
        m = jnp.max(half_max, axis=-1, keepdims=True)
        p = jnp.exp2(s - m).astype(BF16)
        v_ext = jnp.concatenate([vb.astype(BF16), jnp.ones((nk, LANES), BF16)], axis=1)
        o = jnp.dot(p, v_ext, preferred_element_type=F32)
        acc = jnp.where(head0, o[:blk, :LANES], o[blk:, :LANES])
        l = jnp.where(head0, o[:blk, LANES:], o[blk:, LANES:])
        return acc / l, jnp.where(head0, m[:blk], m[blk:]) + jnp.log2(l)

    def run_unit(br, src, q_rows, kv_rows, out_rows):
        qs, ks, vs = src
        on, lse = unit(qs(q_rows), ks(kv_rows), vs(kv_rows))
        on_ref[br, out_rows, :] = on
        lse_ref[br, out_rows, :] = lse

    natural = tuple((lambda rows, r=r: r[0, 0, rows, :]) for r in (q_ref, k_ref, v_ref))
    residue_major = tuple((lambda rows, r=r: r[rows, :]) for r in (q4, k4, v4))

    for r in range(4):
        for src, dst in ((q_ref, q4), (k_ref, k4), (v_ref, v4)):
            dst[pl.ds(r * sub, sub), :] = src[0, 0, pl.ds(r, sub, stride=4), :]

    for n in range(seq // blk):
        q_rows = pl.ds(n * blk, blk)
        run_unit(0, natural, q_rows, q_rows if n == 0 else pl.ds((n - 1) * blk, 2 * blk), q_rows)
    for r in range(4):
        for n in range(sub // blk):
            q_rows = pl.ds(r * sub + n * blk, blk)
            run_unit(1, residue_major, q_rows,
                     q_rows if n == 0 else pl.ds(r * sub + (n - 1) * blk, 2 * blk), q_rows)
    assert sub // 4 == blk
    for r16 in range(16):
        rows = pl.ds((r16 % 4) * sub + r16 // 4, blk, stride=4)
        run_unit(2, residue_major, rows, rows, rows)

    for r in range(4):
        nat = pl.ds(r, sub, stride=4)
        rm = pl.ds(r * sub, sub)
        l0, l1, l2 = lse_ref[0, nat, :], lse_ref[1, rm, :], lse_ref[2, rm, :]
        mx = jnp.maximum(jnp.maximum(l0, l1), l2)
        w0, w1, w2 = jnp.exp2(l0 - mx), jnp.exp2(l1 - mx), jnp.exp2(l2 - mx)
        num = w0 * on_ref[0, nat, :] + w1 * on_ref[1, rm, :] + w2 * on_ref[2, rm, :]
        o_ref[0, nat, :] = num / (w0 + w1 + w2)


def _attention(q, k, v):
    b, n_slabs, s, _ = q.shape
    spec = pl.BlockSpec((1, 1, s, LANES), lambda i, j: (i, j, 0, 0))
    return pl.pallas_call(
        functools.partial(_attn_kernel, seq=s),
        grid=(b, n_slabs),
        in_specs=[spec, spec, spec],
        out_specs=pl.BlockSpec((1, s, LANES), lambda i, j: (i, 0, j)),
        out_shape=jax.ShapeDtypeStruct((b, s, n_slabs * LANES), F32),
        scratch_shapes=[pltpu.VMEM((s, LANES), F32)] * 3 + [pltpu.VMEM((3, s, LANES), F32)] * 2,
        compiler_params=_params("parallel", "parallel"),
        name="dilated_attention",
    )(q, k, v)


def _s5_kernel(u_ref, bm_ref, are_ref, aim_ref, cm_ref, d_ref, o_ref, utb, hs, hcar, ys, *, ts):
    @pl.when(pl.program_id(2) == 0)
    def _():
        hcar[...] = jnp.zeros_like(hcar)

    for b in range(SUBLANES):
        utb[pl.ds(b, ts, stride=SUBLANES), :] = u_ref[b]
    u_tb = utb[...]
    hs[...] = jnp.dot(u_tb.astype(BF16), bm_ref[0], preferred_element_type=F32)
    ar = jnp.broadcast_to(are_ref[0], (SUBLANES, SLAB_STATE))
    ai = jnp.broadcast_to(aim_ref[0], (SUBLANES, SLAB_STATE))

    def step(t, carry):
        hr, hi = carry
        row = pl.multiple_of(t * SUBLANES, SUBLANES)
        xr = hs[pl.ds(row, SUBLANES), :SLAB_STATE]
        xi = hs[pl.ds(row, SUBLANES), SLAB_STATE:]
        nr = ar * hr - ai * hi + xr
        ni = ar * hi + ai * hr + xi
        hs[pl.ds(row, SUBLANES), :SLAB_STATE] = nr
        hs[pl.ds(row, SUBLANES), SLAB_STATE:] = ni
        return nr, ni

    hr, hi = lax.fori_loop(0, ts, step, (hcar[:, :SLAB_STATE], hcar[:, SLAB_STATE:]), unroll=True)
    hcar[:, :SLAB_STATE] = hr
    hcar[:, SLAB_STATE:] = hi

    y = jnp.dot(hs[...].astype(BF16), cm_ref[0], preferred_element_type=F32)
    ys[...] = jax.nn.gelu(y + d_ref[0] * u_tb)
    for b in range(SUBLANES):
        o_ref[b] = ys[pl.ds(b, ts, stride=SUBLANES), :]


def _s5(u, bmat, a_re, a_im, cmat, dskip, ts):
    b, s, w = u.shape
    rows = SUBLANES * ts
    return pl.pallas_call(
        functools.partial(_s5_kernel, ts=ts),
        grid=(b // SUBLANES, N_SLABS, s // ts),
        in_specs=[
            pl.BlockSpec((SUBLANES, ts, LANES), lambda i, j, t: (i, t, j)),
            pl.BlockSpec((1, LANES, 2 * SLAB_STATE), lambda i, j, t: (j, 0, 0)),
            pl.BlockSpec((1, 1, SLAB_STATE), lambda i, j, t: (j, 0, 0)),
            pl.BlockSpec((1, 1, SLAB_STATE), lambda i, j, t: (j, 0, 0)),
            pl.BlockSpec((1, 2 * SLAB_STATE, LANES), lambda i, j, t: (j, 0, 0)),
            pl.BlockSpec((1, 1, LANES), lambda i, j, t: (j, 0, 0)),
        ],
        out_specs=pl.BlockSpec((SUBLANES, ts, LANES), lambda i, j, t: (i, t, j)),
        out_shape=jax.ShapeDtypeStruct((b, s, w), F32),
        scratch_shapes=[
            pltpu.VMEM((rows, LANES), F32),
            pltpu.VMEM((rows, 2 * SLAB_STATE), F32),
            pltpu.VMEM((SUBLANES, 2 * SLAB_STATE), F32),
            pltpu.VMEM((rows, LANES), F32),
        ],
        compiler_params=_params("parallel", "parallel", "arbitrary"),
        name="s5_scan",
    )(u, bmat, a_re, a_im, cmat, dskip)


def _s5_params(a_re, a_im, log_dt, b_re, b_im, c_re, c_im, d_skip):
    a = lax.complex(a_re.astype(F32), a_im.astype(F32))
    dt = jnp.exp(log_dt.astype(F32))[:, None]
    a_bar = jnp.exp(a * dt)
    b_bar = ((a_bar - 1.0) / a)[..., None] * lax.complex(b_re.astype(F32), b_im.astype(F32))
    eye = jnp.eye(GROUPS_PER_SLAB, dtype=F32)

    def block_diag(m):
        g, r, c = m.shape
        m = m.reshape(N_SLABS, GROUPS_PER_SLAB, r, c)
        return jnp.einsum('sgrc,gh->sgrhc', m, eye).reshape(N_SLABS, GROUPS_PER_SLAB * r, GROUPS_PER_SLAB * c)

    b_t = jnp.swapaxes(b_bar, 1, 2)
    bmat = jnp.concatenate([block_diag(jnp.real(b_t)), block_diag(jnp.imag(b_t))], axis=-1)
    c_re_t = jnp.swapaxes(c_re.astype(F32), 1, 2)
    c_im_t = jnp.swapaxes(c_im.astype(F32), 1, 2)
    cmat = jnp.concatenate([block_diag(c_re_t), block_diag(-c_im_t)], axis=1)
    return (bmat.astype(BF16),
            jnp.real(a_bar).reshape(N_SLABS, 1, SLAB_STATE),
            jnp.imag(a_bar).reshape(N_SLABS, 1, SLAB_STATE),
            cmat.astype(BF16),
            d_skip.astype(F32).reshape(N_SLABS, 1, LANES))


def _ffn_block(x, g_pre, w_gate, w_up, w_down, g_post):
    hn = _rms_norm(x, g_pre).astype(BF16)
    gate = jnp.dot(hn, w_gate, preferred_element_type=F32)
    up = jnp.dot(hn, w_up, preferred_element_type=F32)
    act = (jax.nn.silu(gate) * up).astype(BF16)
    ff = jnp.dot(act, w_down, preferred_element_type=F32)
    return x + _rms_norm(ff, g_post)


def _ffn_kernel(x_ref, gpre_ref, wg_ref, wu_ref, wd_ref, gpost_ref, o_ref):
    o_ref[...] = _ffn_block(x_ref[...], gpre_ref[...], wg_ref[...], wu_ref[...], wd_ref[...], gpost_ref[...])


def _ffn(x, g_pre, w_gate, w_up, w_down, g_post, tm):
    t, d = x.shape
    row_spec = pl.BlockSpec((tm, d), lambda i: (i, 0))
    return pl.pallas_call(
        _ffn_kernel,
        grid=(t // tm,),
        in_specs=[row_spec, _const_spec((1, d)), _const_spec(w_gate.shape), _const_spec(w_up.shape),
                  _const_spec(w_down.shape), _const_spec((1, d))],
        out_specs=row_spec,
        out_shape=jax.ShapeDtypeStruct((t, d), F32),
        compiler_params=_params("parallel"),
        name="ffn",
    )(x, g_pre, w_gate, w_up, w_down, g_post)


def _even_post_kernel(att_ref, yg_ref, x_ref, wglu_ref, wout_ref, gpost_ref,
                      gfpre_ref, wg_ref, wu_ref, wd_ref, gfpost_ref, o_ref):
    yg = yg_ref[...]
    glu = jnp.dot(yg.astype(BF16), wglu_ref[...], preferred_element_type=F32)
    ssm = (yg * jax.nn.sigmoid(glu)).astype(BF16)
    mix = (jnp.dot(att_ref[...].astype(BF16), wout_ref[:ATT_WIDTH, :], preferred_element_type=F32)
           + jnp.dot(ssm, wout_ref[ATT_WIDTH:, :], preferred_element_type=F32))
    x1 = x_ref[...] + _rms_norm(mix, gpost_ref[...])
    o_ref[...] = _ffn_block(x1, gfpre_ref[...], wg_ref[...], wu_ref[...], wd_ref[...], gfpost_ref[...])


def _even_post_ffn(att, yg, x, w_glu, w_out, g_post, g_fpre, w_gate, w_up, w_down, g_fpost, tm):
    t, d = x.shape
    row = lambda w: pl.BlockSpec((tm, w), lambda i: (i, 0))
    return pl.pallas_call(
        _even_post_kernel,
        grid=(t // tm,),
        in_specs=[row(ATT_WIDTH), row(SSM_WIDTH), row(d), _const_spec(w_glu.shape), _const_spec(w_out.shape),
                  _const_spec((1, d)), _const_spec((1, d)), _const_spec(w_gate.shape),
                  _const_spec(w_up.shape), _const_spec(w_down.shape), _const_spec((1, d))],
        out_specs=row(d),
        out_shape=jax.ShapeDtypeStruct((t, d), F32),
        compiler_params=_params("parallel"),
        name="even_post_ffn",
    )(att, yg, x, w_glu, w_out, g_post, g_fpre, w_gate, w_up, w_down, g_fpost)


def _rglru_kernel(x_ref, gpre_ref, win_ref, convw_ref, convb_ref, wr_ref, br_ref, wi_ref, bi_ref, lam_ref,
                  wout_ref, gpost_ref, o_ref, hn_tb, xprev, hcar, a_s, b_s, mix_tb, *, ts):
    rows = SUBLANES * ts
    n_slabs = LRU_WIDTH // LANES
    tail = (CONV_WIDTH - 1) * SUBLANES

    @pl.when(pl.program_id(1) == 0)
    def _():
        xprev[...] = jnp.zeros_like(xprev)
        hcar[...] = jnp.zeros_like(hcar)

    for b in range(SUBLANES):
        hn = _rms_norm(x_ref[b], gpre_ref[...])
        for j in range(n_slabs):
            hn_tb[j, pl.ds(b, ts, stride=SUBLANES), :] = hn[:, j * LANES:(j + 1) * LANES]
    hn_rows = jnp.concatenate([hn_tb[j] for j in range(n_slabs)], axis=1).astype(BF16)
    z = jnp.dot(hn_rows, win_ref[...], preferred_element_type=F32)
    xb = z[:, :LRU_WIDTH]
    gate_branch = z[:, LRU_WIDTH:]

    xext = jnp.concatenate([xprev[...], xb], axis=0)
    xc = convb_ref[...]
    for kk in range(CONV_WIDTH):
        xc = xc + convw_ref[kk:kk + 1, :] * xext[kk * SUBLANES:kk * SUBLANES + rows]
    xprev[...] = xb[rows - tail:]

    xcb = xc.astype(BF16)

    def block_gates(w_ref, bias_ref):
        parts = [jnp.dot(xcb[:, h * LRU_BLOCK_DIM:(h + 1) * LRU_BLOCK_DIM], w_ref[h],
                         preferred_element_type=F32) for h in range(LRU_BLOCKS)]
        return 0.5 * jnp.tanh(0.5 * (jnp.concatenate(parts, axis=1) + bias_ref[...])) + 0.5

    gate_r = block_gates(wr_ref, br_ref)
    gate_i = block_gates(wi_ref, bi_ref)
    log_a = gate_r * (-RG_C * jax.nn.softplus(-lam_ref[...]))
    a = jnp.exp(log_a)
    v = -jnp.tanh(log_a) * (1.0 + a * a)
    mult = jnp.where(v > 0.0, v * lax.rsqrt(v), 0.0)
    a_s[...] = a
    b_s[...] = mult * (gate_i * xc)

    def step(t, h):
        row = pl.multiple_of(t * SUBLANES, SUBLANES)
        h = a_s[pl.ds(row, SUBLANES), :] * h + b_s[pl.ds(row, SUBLANES), :]
        b_s[pl.ds(row, SUBLANES), :] = h
        return h

    hcar[...] = lax.fori_loop(0, ts, step, hcar[...], unroll=True)

    y = (b_s[...] * jax.nn.gelu(gate_branch)).astype(BF16)
    mix = _rms_norm(jnp.dot(y, wout_ref[...], preferred_element_type=F32), gpost_ref[...])
    for j in range(n_slabs):
        mix_tb[j] = mix[:, j * LANES:(j + 1) * LANES]
    for b in range(SUBLANES):
        mix_b = jnp.concatenate([mix_tb[j, pl.ds(b, ts, stride=SUBLANES), :] for j in range(n_slabs)], axis=1)
        o_ref[b] = x_ref[b] + mix_b


def _rglru(x, g_pre, w_in, conv_w, conv_b, w_r, b_r, w_i, b_i, lam, w_out, g_post, ts):
    b, s, d = x.shape
    rows = SUBLANES * ts
    blk = pl.BlockSpec((SUBLANES, ts, d), lambda i, t: (i, t, 0))
    return pl.pallas_call(
        functools.partial(_rglru_kernel, ts=ts),
        grid=(b // SUBLANES, s // ts),
        in_specs=[blk, _const_spec((1, d)), _const_spec(w_in.shape), _const_spec(conv_w.shape),
                  _const_spec((1, d)), _const_spec(w_r.shape), _const_spec((1, d)), _const_spec(w_i.shape),
                  _const_spec((1, d)), _const_spec((1, d)), _const_spec(w_out.shape), _const_spec((1, d))],
        out_specs=blk,
        out_shape=jax.ShapeDtypeStruct((b, s, d), F32),
        scratch_shapes=[
            pltpu.VMEM((d // LANES, rows, LANES), F32),
            pltpu.VMEM(((CONV_WIDTH - 1) * SUBLANES, LRU_WIDTH), F32),
            pltpu.VMEM((SUBLANES, LRU_WIDTH), F32),
            pltpu.VMEM((rows, LRU_WIDTH), F32),
            pltpu.VMEM((rows, LRU_WIDTH), F32),
            pltpu.VMEM((d // LANES, rows, LANES), F32),
        ],
        compiler_params=_params("parallel", "arbitrary"),
        name="rglru_block",
    )(x, g_pre, w_in, conv_w, conv_b, w_r, b_r, w_i, b_i, lam, w_out, g_post)


def _tiles(s):
    return dict(even_in_rows=min(512, s), s5_steps=min(256, s), ffn_rows=512, rglru_steps=min(64, s))


def kernel(x, positions, norm_mix_pre, norm_mix_post, norm_ffn_pre, norm_ffn_post, ev_w_in, ev_w_out,
           s5_a_re, s5_a_im, s5_log_dt, s5_b_re, s5_b_im, s5_c_re, s5_c_im, s5_d, s5_w_glu,
           od_w_in, od_w_out, rg_conv_w, rg_conv_b, rg_w_r, rg_b_r, rg_w_i, rg_b_i, rg_lam,
           ffn_w_gate, ffn_w_up, ffn_w_down):
    b, s, d = x.shape
    assert d == D_MODEL and b % SUBLANES == 0 and s % (16 * ATT_BLOCK) == 0
    tiles = _tiles(s)
    row = lambda v: v.astype(F32).reshape(1, -1)
    bf = lambda w: w.astype(BF16)

    q, k, v, u = _even_in(x, positions, row(norm_mix_pre[0]), bf(ev_w_in[0]), tiles["even_in_rows"])
    att = _attention(q, k, v)
    yg = _s5(u, *_s5_params(s5_a_re[0], s5_a_im[0], s5_log_dt[0], s5_b_re[0], s5_b_im[0],
                            s5_c_re[0], s5_c_im[0], s5_d[0]), tiles["s5_steps"])
    x2 = _even_post_ffn(att.reshape(b * s, ATT_WIDTH), yg.reshape(b * s, SSM_WIDTH), x.reshape(b * s, d),
                        bf(s5_w_glu[0]), bf(ev_w_out[0]), row(norm_mix_post[0]),
                        row(norm_ffn_pre[0]), bf(ffn_w_gate[0]), bf(ffn_w_up[0]), bf(ffn_w_down[0]),
                        row(norm_ffn_post[0]), tiles["ffn_rows"])

    x3 = _rglru(x2.reshape(b, s, d), row(norm_mix_pre[1]), bf(od_w_in[0]), rg_conv_w[0].astype(F32),
                row(rg_conv_b[0]), bf(rg_w_r[0]), row(rg_b_r[0]), bf(rg_w_i[0]), row(rg_b_i[0]),
                row(rg_lam[0]), bf(od_w_out[0]), row(norm_mix_post[1]), tiles["rglru_steps"])
    x4 = _ffn(x3.reshape(b * s, d), row(norm_ffn_pre[1]), bf(ffn_w_gate[1]), bf(ffn_w_up[1]),
              bf(ffn_w_down[1]), row(norm_ffn_post[1]), tiles["ffn_rows"])
    return x4.reshape(b, s, d)
```

```python
import functools
import math

import jax
import jax.numpy as jnp
from jax import lax
from jax.experimental import pallas as pl
from jax.experimental.pallas import tpu as pltpu

F32 = jnp.float32
BF16 = jnp.bfloat16

D_MODEL = 1024
HEAD_DIM = 64
ATT_WIDTH = 512
ROPE_DIM = 16
ROPE_THETA = 500000.0
ATT_BLOCK = 128
NEG_INF = -1e30
SSM_WIDTH = 512
SSM_GROUP = 16
SSM_GROUPS = 32
SSM_STATE = 64
LRU_WIDTH = 1024
LRU_BLOCKS = 4
LRU_BLOCK_DIM = 256
CONV_WIDTH = 4
RG_C = 8.0
FFN_HIDDEN = 2816
NORM_EPS = 1e-6

LANES = 128
SUBLANES = 8
N_SLABS = ATT_WIDTH // LANES
GROUPS_PER_SLAB = LANES // SSM_GROUP
SLAB_STATE = GROUPS_PER_SLAB * SSM_STATE
GELU_K1 = 2.0 * math.sqrt(2.0 / math.pi)
GELU_K3 = 8.0 * math.sqrt(2.0 / math.pi) * 0.044715
Q_SCALE =HEAD_DIM ** -0.5 * math.log2(math.e)
VMEM_LIMIT_BYTES = 56 * 1024 * 1024


def _rms_norm(x, g):
    return x * lax.rsqrt(jnp.mean(x * x, axis=-1, keepdims=True) + NORM_EPS) * g


def _const_spec(shape):
    n = len(shape)
    return pl.BlockSpec(shape, lambda *_: (0,) * n, pipeline_mode=pl.Buffered(1))


def _params(*sem):
    return pltpu.CompilerParams(dimension_semantics=sem, vmem_limit_bytes=VMEM_LIMIT_BYTES)


def _even_in_kernel(x_ref, pos_ref, g_ref, w_ref, invf_ref, sgn_ref, q_ref, k_ref, v_ref, u_ref, cos_s, sin_s):
    quarter = pos_ref.shape[1]
    ang = pos_ref[0].astype(F32) * invf_ref[...]
    cos4 = jnp.cos(ang)
    sin4 = jnp.sin(ang) * sgn_ref[...]
    lane128 = lax.broadcasted_iota(jnp.int32, (1, LANES), 1)
    rot_lo = lane128 < ROPE_DIM
    rot_hi = (lane128 >= HEAD_DIM) & (lane128 < HEAD_DIM + ROPE_DIM)
    for r in range(4):
        for packed, dst, ident in ((cos4, cos_s, 1.0), (sin4, sin_s, 0.0)):
            t = packed if r == 0 else pltpu.roll(packed, LANES - 32 * r, 1)
            t = jnp.where(rot_lo, t, ident)
            t = jnp.where(rot_hi, pltpu.roll(t, HEAD_DIM, 1), t)
            dst[pl.ds(r, quarter, stride=4), :] = t
    cos = cos_s[...]
    sin = sin_s[...]
    hn = _rms_norm(x_ref[0], g_ref[...])
    z = jnp.dot(hn.astype(BF16), w_ref[...], preferred_element_type=F32)
    lane = lax.broadcasted_iota(jnp.int32, (1, LANES), 1) % HEAD_DIM
    first_half = lane < (ROPE_DIM // 2)

    def rope(t):
        partner = jnp.where(first_half, pltpu.roll(t, LANES - ROPE_DIM // 2, 1),
                            pltpu.roll(t, ROPE_DIM // 2, 1))
        return t * cos + partner * sin

    for j in range(N_SLABS):
        lo = j * LANES
        q_ref[0, j] = rope(z[:, lo:lo + LANES]) * Q_SCALE
        k_ref[0, j] = rope(z[:, ATT_WIDTH + lo:ATT_WIDTH + lo + LANES])
        v_ref[0, j] = z[:, 2 * ATT_WIDTH + lo:2 * ATT_WIDTH + lo + LANES]
    u_ref[0] = z[:, 3 * ATT_WIDTH:]


def _even_in(x, positions, g, w_in, tm):
    b, s, d = x.shape
    half = ROPE_DIM // 2
    inv_freq = ROPE_THETA ** (-(jnp.arange(half, dtype=F32) * 2.0 / ROPE_DIM))
    lane = jnp.arange(LANES) % 32
    invf = jnp.where(lane < ROPE_DIM, inv_freq[lane % half], 0.0).astype(F32)[None, :]
    sgn = jnp.where(lane < half, -1.0, jnp.where(lane < ROPE_DIM, 1.0, 0.0)).astype(F32)[None, :]
    pos4 = jnp.repeat(positions.reshape(b, s // 4, 4), 32, axis=-1)
    slab = jax.ShapeDtypeStruct((b, N_SLABS, s, LANES), F32)
    slab_spec = pl.BlockSpec((1, N_SLABS, tm, LANES), lambda i, j: (i, 0, j, 0))
    return pl.pallas_call(
        _even_in_kernel,
        grid=(b, s // tm),
        in_specs=[
            pl.BlockSpec((1, tm, d), lambda i, j: (i, j, 0)),
            pl.BlockSpec((1, tm // 4, LANES), lambda i, j: (i, j, 0)),
            _const_spec((1, d)),
            _const_spec(w_in.shape),
            _const_spec((1, LANES)),
            _const_spec((1, LANES)),
        ],
        out_specs=[slab_spec, slab_spec, slab_spec,
                   pl.BlockSpec((1, tm, SSM_WIDTH), lambda i, j: (i, j, 0))],
        out_shape=[slab, slab, slab, jax.ShapeDtypeStruct((b, s, SSM_WIDTH), F32)],
        scratch_shapes=[pltpu.VMEM((tm, LANES), F32)] * 2,
        compiler_params=_params("parallel", "parallel"),
        name="even_in",
    )(x, pos4, g, w_in, invf, sgn)


def _attn_kernel(q_ref, k_ref, v_ref, o_ref, q4, k4, v4, on_ref, lse_ref, *, seq):
    blk = ATT_BLOCK
    sub = seq // 4
    head0 = lax.broadcasted_iota(jnp.int32, (1, LANES), 1) < HEAD_DIM

    def bias(nk):
        qi = lax.broadcasted_iota(jnp.int32, (blk, nk), 0) + (nk - blk)
        ki = lax.broadcasted_iota(jnp.int32, (blk, nk), 1)
        dist = qi - ki
        return jnp.where((dist >= 0) & (dist <= blk), 0.0, NEG_INF).astype(F32)

    biases = {blk: bias(blk), 2 * blk: bias(2 * blk)}

    def unit(qb, kb, vb):
        nk = kb.shape[0]
        k16 = kb.astype(BF16)
        v_ext = jnp.concatenate([vb.astype(BF16), jnp.ones((nk, LANES), BF16)], axis=1)
        outs = []
        for qh in (jnp.where(head0, qb, 0.0), jnp.where(head0, 0.0, qb)):
            s = lax.dot_general(qh.astype(BF16), k16, (((1,), (1,)), ((), ())), preferred_element_type=F32)
            s = s + biases[nk]
            half_max = s if nk == blk else jnp.maximum(s[:, :blk], s[:, blk:])
            m = jnp.max(half_max, axis=-1, keepdims=True)
            p = jnp.exp2(s - m).astype(BF16)
            outs.append((jnp.dot(p, v_ext, preferred_element_type=F32), m))
        (o0, m0), (o1, m1) = outs
        acc = jnp.where(head0, o0[:, :LANES], o1[:, :LANES])
        l = jnp.where(head0, o0[:, LANES:], o1[:, LANES:])
        return acc / l, jnp.where(head0, m0, m1) + jnp.log2(l)

    def run_unit(br, src, q_rows, kv_rows, out_rows):
        qs, ks, vs = src
        on, lse = unit(qs(q_rows), ks(kv_rows), vs(kv_rows))
        on_ref[br, out_rows, :] = on
        lse_ref[br, out_rows, :] = lse

    natural = tuple((lambda rows, r=r: r[0, 0, rows, :]) for r in (q_ref, k_ref, v_ref))
    residue_major = tuple((lambda rows, r=r: r[rows, :]) for r in (q4, k4, v4))

    for r in range(4):
        for src, dst in ((q_ref, q4), (k_ref, k4), (v_ref, v4)):
            dst[pl.ds(r * sub, sub), :] = src[0, 0, pl.ds(r, sub, stride=4), :]

    for n in range(seq // blk):
        q_rows = pl.ds(n * blk, blk)
        run_unit(0, natural, q_rows, q_rows if n == 0 else pl.ds((n - 1) * blk, 2 * blk), q_rows)
    for r in range(4):
        for n in range(sub // blk):
            q_rows = pl.ds(r * sub + n * blk, blk)
            run_unit(1, residue_major, q_rows,
                     q_rows if n == 0 else pl.ds(r * sub + (n - 1) * blk, 2 * blk), q_rows)
    assert sub // 4 == blk
    for r16 in range(16):
        rows = pl.ds((r16 % 4) * sub + r16 // 4, blk, stride=4)
        run_unit(2, residue_major, rows, rows, rows)

    for r in range(4):
        nat = pl.ds(r, sub, stride=4)
        rm = pl.ds(r * sub, sub)
        l0, l1, l2 = lse_ref[0, nat, :], lse_ref[1, rm, :], lse_ref[2, rm, :]
        mx = jnp.maximum(jnp.maximum(l0, l1), l2)
        w0, w1, w2 = jnp.exp2(l0 - mx), jnp.exp2(l1 - mx), jnp.exp2(l2 - mx)
        num = w0 * on_ref[0, nat, :] + w1 * on_ref[1, rm, :] + w2 * on_ref[2, rm, :]
        o_ref[0, nat, :] = num / (w0 + w1 + w2)


def _attention(q, k, v):
    b, n_slabs, s, _ = q.shape
    spec = pl.BlockSpec((1, 1, s, LANES), lambda i, j: (i, j, 0, 0))
    return pl.pallas_call(
        functools.partial(_attn_kernel, seq=s),
        grid=(b, n_slabs),
        in_specs=[spec, spec, spec],
        out_specs=pl.BlockSpec((1, s, LANES), lambda i, j: (i, 0, j)),
        out_shape=jax.ShapeDtypeStruct((b, s, n_slabs * LANES), F32),
        scratch_shapes=[pltpu.VMEM((s, LANES), F32)] * 3 + [pltpu.VMEM((3, s, LANES), F32)] * 2,
        compiler_params=_params("parallel", "parallel"),
        name="dilated_attention",
    )(q, k, v)


def _s5_kernel(u_ref, bm_ref, are_ref, aim_ref, cm_ref, d_ref, o_ref, utb, hs, hcar, ys, *, ts):
    @pl.when(pl.program_id(2) == 0)
    def _():
        hcar[...] = jnp.zeros_like(hcar)

    for b in range(SUBLANES):
        utb[pl.ds(b, ts, stride=SUBLANES), :] = u_ref[b]
    u_tb = utb[...]
    hs[...] = jnp.dot(u_tb.astype(BF16), bm_ref[0], preferred_element_type=F32)
    ar = jnp.broadcast_to(are_ref[0], (SUBLANES, SLAB_STATE))
    ai = jnp.broadcast_to(aim_ref[0], (SUBLANES, SLAB_STATE))

    def step(t, carry):
        hr, hi = carry
        row = pl.multiple_of(t * SUBLANES, SUBLANES)
        xr = hs[pl.ds(row, SUBLANES), :SLAB_STATE]
        xi = hs[pl.ds(row, SUBLANES), SLAB_STATE:]
        nr = ar * hr - ai * hi + xr
        ni = ar * hi + ai * hr + xi
        hs[pl.ds(row, SUBLANES), :SLAB_STATE] = nr
        hs[pl.ds(row, SUBLANES), SLAB_STATE:] = ni
        return nr, ni

    hr, hi = lax.fori_loop(0, ts, step, (hcar[:, :SLAB_STATE], hcar[:, SLAB_STATE:]), unroll=True)
    hcar[:, :SLAB_STATE] = hr
    hcar[:, SLAB_STATE:] = hi

    y = jnp.dot(hs[...].astype(BF16), cm_ref[0], preferred_element_type=F32)
    ys[...] = jax.nn.gelu(y + d_ref[0] * u_tb)
    for b in range(SUBLANES):
        o_ref[b] = ys[pl.ds(b, ts, stride=SUBLANES), :]


def _s5(u, bmat, a_re, a_im, cmat, dskip, ts):
    b, s, w = u.shape
    rows = SUBLANES * ts
    return pl.pallas_call(
        functools.partial(_s5_kernel, ts=ts),
        grid=(b // SUBLANES, N_SLABS, s // ts),
        in_specs=[
            pl.BlockSpec((SUBLANES, ts, LANES), lambda i, j, t: (i, t, j)),
            pl.BlockSpec((1, LANES, 2 * SLAB_STATE), lambda i, j, t: (j, 0, 0)),
            pl.BlockSpec((1, 1, SLAB_STATE), lambda i, j, t: (j, 0, 0)),
            pl.BlockSpec((1, 1, SLAB_STATE), lambda i, j, t: (j, 0, 0)),
            pl.BlockSpec((1, 2 * SLAB_STATE, LANES), lambda i, j, t: (j, 0, 0)),
            pl.BlockSpec((1, 1, LANES), lambda i, j, t: (j, 0, 0)),
        ],
        out_specs=pl.BlockSpec((SUBLANES, ts, LANES), lambda i, j, t: (i, t, j)),
        out_shape=jax.ShapeDtypeStruct((b, s, w), F32),
        scratch_shapes=[
            pltpu.VMEM((rows, LANES), F32),
            pltpu.VMEM((rows, 2 * SLAB_STATE), F32),
            pltpu.VMEM((SUBLANES, 2 * SLAB_STATE), F32),
            pltpu.VMEM((rows, LANES), F32),
        ],
        compiler_params=_params("parallel", "parallel", "arbitrary"),
        name="s5_scan",
    )(u, bmat, a_re, a_im, cmat, dskip)


def _s5_params(a_re, a_im, log_dt, b_re, b_im, c_re, c_im, d_skip):
    a = lax.complex(a_re.astype(F32), a_im.astype(F32))
    dt = jnp.exp(log_dt.astype(F32))[:, None]
    a_bar = jnp.exp(a * dt)
    b_bar = ((a_bar - 1.0) / a)[..., None] * lax.complex(b_re.astype(F32), b_im.astype(F32))
    eye = jnp.eye(GROUPS_PER_SLAB, dtype=F32)

    def block_diag(m):
        g, r, c = m.shape
        m = m.reshape(N_SLABS, GROUPS_PER_SLAB, r, c)
        return jnp.einsum('sgrc,gh->sgrhc', m, eye).reshape(N_SLABS, GROUPS_PER_SLAB * r, GROUPS_PER_SLAB * c)

    b_t = jnp.swapaxes(b_bar, 1, 2)
    bmat = jnp.concatenate([block_diag(jnp.real(b_t)), block_diag(jnp.imag(b_t))], axis=-1)
    c_re_t = jnp.swapaxes(c_re.astype(F32), 1, 2)
    c_im_t = jnp.swapaxes(c_im.astype(F32), 1, 2)
    cmat = jnp.concatenate([block_diag(c_re_t), block_diag(-c_im_t)], axis=1)
    return (bmat.astype(BF16),
            jnp.real(a_bar).reshape(N_SLABS, 1, SLAB_STATE),
            jnp.imag(a_bar).reshape(N_SLABS, 1, SLAB_STATE),
            cmat.astype(BF16),
            d_skip.astype(F32).reshape(N_SLABS, 1, LANES))


def _ffn_block(x, g_pre, w_gate, w_up, w_down, g_post):
    hn = _rms_norm(x, g_pre).astype(BF16)
    gate = jnp.dot(hn, w_gate, preferred_element_type=F32)
    up = jnp.dot(hn, w_up, preferred_element_type=F32)
    act = (jax.nn.silu(gate) * up).astype(BF16)
    ff = jnp.dot(act, w_down, preferred_element_type=F32)
    return x + _rms_norm(ff, g_post)


def _ffn_kernel(x_ref, gpre_ref, wg_ref, wu_ref, wd_ref, gpost_ref, o_ref):
    o_ref[...] = _ffn_block(x_ref[...], gpre_ref[...], wg_ref[...], wu_ref[...], wd_ref[...], gpost_ref[...])


def _ffn(x, g_pre, w_gate, w_up, w_down, g_post, tm):
    t, d = x.shape
    row_spec = pl.BlockSpec((tm, d), lambda i: (i, 0))
    return pl.pallas_call(
        _ffn_kernel,
        grid=(t // tm,),
        in_specs=[row_spec, _const_spec((1, d)), _const_spec(w_gate.shape), _const_spec(w_up.shape),
                  _const_spec(w_down.shape), _const_spec((1, d))],
        out_specs=row_spec,
        out_shape=jax.ShapeDtypeStruct((t, d), F32),
        compiler_params=_params("parallel"),
        name="ffn",
    )(x, g_pre, w_gate, w_up, w_down, g_post)


def _even_post_kernel(att_ref, yg_ref, x_ref, wglu_ref, wout_ref, gpost_ref,
                      gfpre_ref, wg_ref, wu_ref, wd_ref, gfpost_ref, o_ref):
    yg = yg_ref[...]
    glu = jnp.dot(yg.astype(BF16), wglu_ref[...], preferred_element_type=F32)
    ssm = (yg * jax.nn.sigmoid(glu)).astype(BF16)
    mix = (jnp.dot(att_ref[...].astype(BF16), wout_ref[:ATT_WIDTH, :], preferred_element_type=F32)
           + jnp.dot(ssm, wout_ref[ATT_WIDTH:, :], preferred_element_type=F32))
    x1 = x_ref[...] + _rms_norm(mix, gpost_ref[...])
    o_ref[...] = _ffn_block(x1, gfpre_ref[...], wg_ref[...], wu_ref[...], wd_ref[...], gfpost_ref[...])


def _even_post_ffn(att, yg, x, w_glu, w_out, g_post, g_fpre, w_gate, w_up, w_down, g_fpost, tm):
    t, d = x.shape
    row = lambda w: pl.BlockSpec((tm, w), lambda i: (i, 0))
    return pl.pallas_call(
        _even_post_kernel,
        grid=(t // tm,),
        in_specs=[row(ATT_WIDTH), row(SSM_WIDTH), row(d), _const_spec(w_glu.shape), _const_spec(w_out.shape),
                  _const_spec((1, d)), _const_spec((1, d)), _const_spec(w_gate.shape),
                  _const_spec(w_up.shape), _const_spec(w_down.shape), _const_spec((1, d))],
        out_specs=row(d),
        out_shape=jax.ShapeDtypeStruct((t, d), F32),
        compiler_params=_params("parallel"),
        name="even_post_ffn",
    )(att, yg, x, w_glu, w_out, g_post, g_fpre, w_gate, w_up, w_down, g_fpost)


def _rglru_kernel(x_ref, gpre_ref, win_ref, convw_ref, convb_ref, wr_ref, br_ref, wi_ref, bi_ref, lam_ref,
                  wout_ref, gpost_ref, o_ref, hn_tb, xprev, hcar, a_s, b_s, mix_tb, *, ts):
    rows = SUBLANES * ts
    n_slabs = LRU_WIDTH // LANES
    tail = (CONV_WIDTH - 1) * SUBLANES

    @pl.when(pl.program_id(1) == 0)
    def _():
        xprev[...] = jnp.zeros_like(xprev)
        hcar[...] = jnp.zeros_like(hcar)

    for b in range(SUBLANES):
        hn = _rms_norm(x_ref[b], gpre_ref[...])
        for j in range(n_slabs):
            hn_tb[j, pl.ds(b, ts, stride=SUBLANES), :] = hn[:, j * LANES:(j + 1) * LANES]
    hn_rows = jnp.concatenate([hn_tb[j] for j in range(n_slabs)], axis=1).astype(BF16)
    z = jnp.dot(hn_rows, win_ref[...], preferred_element_type=F32)
    xb = z[:, :LRU_WIDTH]
    gate_half = z[:, LRU_WIDTH:]

    xext = jnp.concatenate([xprev[...], xb], axis=0)
    xc_half = convb_ref[...]
    for kk in range(CONV_WIDTH):
        xc_half = xc_half + convw_ref[kk:kk + 1, :] * xext[kk * SUBLANES:kk * SUBLANES + rows]
    xprev[...] = xb[rows - tail:]

    xcb = xc_half.astype(BF16)

    def block_tanh(w_ref, bias_ref):
        parts = [jnp.dot(xcb[:, h * LRU_BLOCK_DIM:(h + 1) * LRU_BLOCK_DIM], w_ref[h],
                         preferred_element_type=F32) for h in range(LRU_BLOCKS)]
        return jnp.tanh(jnp.concatenate(parts, axis=1) + bias_ref[...])

    t_r = block_tanh(wr_ref, br_ref)
    t_i = block_tanh(wi_ref, bi_ref)
    c_half = (-0.5 * RG_C) * jax.nn.softplus(-lam_ref[...])
    log_a = t_r * c_half + c_half
    a = jnp.exp(log_a)
    v = -jnp.tanh(log_a) * (1.0 + a * a)
    mult = jnp.where(v > 0.0, v * lax.rsqrt(v), 0.0)
    a_s[...] = a
    b_s[...] = (t_i + 1.0) * (xc_half * mult)

    def step(t, h):
        row = pl.multiple_of(t * SUBLANES, SUBLANES)
        h = a_s[pl.ds(row, SUBLANES), :] * h + b_s[pl.ds(row, SUBLANES), :]
        b_s[pl.ds(row, SUBLANES), :] = h
        return h

    hcar[...] = lax.fori_loop(0, ts, step, hcar[...], unroll=True)

    inner = gate_half * (GELU_K1 + GELU_K3 * (gate_half * gate_half))
    y = ((b_s[...] * gate_half) * (1.0 + jnp.tanh(inner))).astype(BF16)
    mix = _rms_norm(jnp.dot(y, wout_ref[...], preferred_element_type=F32), gpost_ref[...])
    for j in range(n_slabs):
        mix_tb[j] = mix[:, j * LANES:(j + 1) * LANES]
    for b in range(SUBLANES):
        mix_b = jnp.concatenate([mix_tb[j, pl.ds(b, ts, stride=SUBLANES), :] for j in range(n_slabs)], axis=1)
        o_ref[b] = x_ref[b] + mix_b


def _rglru(x, g_pre, w_in, conv_w, conv_b, w_r, b_r, w_i, b_i, lam, w_out, g_post, ts):
    b, s, d = x.shape
    rows = SUBLANES * ts
    blk = pl.BlockSpec((SUBLANES, ts, d), lambda i, t: (i, t, 0))
    w_in = jnp.concatenate([w_in[:, :LRU_WIDTH], w_in[:, LRU_WIDTH:] * 0.5], axis=1)
    conv_w, conv_b, b_r, b_i = 0.5 * conv_w, 0.5 * conv_b, 0.5 * b_r, 0.5 * b_i
    return pl.pallas_call(
        functools.partial(_rglru_kernel, ts=ts),
        grid=(b // SUBLANES, s // ts),
        in_specs=[blk, _const_spec((1, d)), _const_spec(w_in.shape), _const_spec(conv_w.shape),
                  _const_spec((1, d)), _const_spec(w_r.shape), _const_spec((1, d)), _const_spec(w_i.shape),
                  _const_spec((1, d)), _const_spec((1, d)), _const_spec(w_out.shape), _const_spec((1, d))],
        out_specs=blk,
        out_shape=jax.ShapeDtypeStruct((b, s, d), F32),
        scratch_shapes=[
            pltpu.VMEM((d // LANES, rows, LANES), F32),
            pltpu.VMEM(((CONV_WIDTH - 1) * SUBLANES, LRU_WIDTH), F32),
            pltpu.VMEM((SUBLANES, LRU_WIDTH), F32),
            pltpu.VMEM((rows, LRU_WIDTH), F32),
            pltpu.VMEM((rows, LRU_WIDTH), F32),
            pltpu.VMEM((d // LANES, rows, LANES), F32),
        ],
        compiler_params=_params("parallel", "arbitrary"),
        name="rglru_block",
    )(x, g_pre, w_in, conv_w, conv_b, w_r, b_r, w_i, b_i, lam, w_out, g_post)


def _tiles(s):
    return dict(even_in_rows=min(512, s), s5_steps=min(256, s), ffn_rows=512, rglru_steps=min(64, s))


def kernel(x, positions, norm_mix_pre, norm_mix_post, norm_ffn_pre, norm_ffn_post, ev_w_in, ev_w_out,
           s5_a_re, s5_a_im, s5_log_dt, s5_b_re, s5_b_im, s5_c_re, s5_c_im, s5_d, s5_w_glu,
           od_w_in, od_w_out, rg_conv_w, rg_conv_b, rg_w_r, rg_b_r, rg_w_i, rg_b_i, rg_lam,
           ffn_w_gate, ffn_w_up, ffn_w_down):
    b, s, d = x.shape
    assert d == D_MODEL and b % SUBLANES == 0 and s % (16 * ATT_BLOCK) == 0
    tiles = _tiles(s)
    row = lambda v: v.astype(F32).reshape(1, -1)
    bf = lambda w: w.astype(BF16)

    q, k, v, u = _even_in(x, positions, row(norm_mix_pre[0]), bf(ev_w_in[0]), tiles["even_in_rows"])
    att = _attention(q, k, v)
    yg = _s5(u, *_s5_params(s5_a_re[0], s5_a_im[0], s5_log_dt[0], s5_b_re[0], s5_b_im[0],
                            s5_c_re[0], s5_c_im[0], s5_d[0]), tiles["s5_steps"])
    x2 = _even_post_ffn(att.reshape(b * s, ATT_WIDTH), yg.reshape(b * s, SSM_WIDTH), x.reshape(b * s, d),
                        bf(s5_w_glu[0]), bf(ev_w_out[0]), row(norm_mix_post[0]),
                        row(norm_ffn_pre[0]), bf(ffn_w_gate[0]), bf(ffn_w_up[0]), bf(ffn_w_down[0]),
                        row(norm_ffn_post[0]), tiles["ffn_rows"])

    x3 = _rglru(x2.reshape(b, s, d), row(norm_mix_pre[1]), bf(od_w_in[0]), rg_conv_w[0].astype(F32),
                row(rg_conv_b[0]), bf(rg_w_r[0]), row(rg_b_r[0]), bf(rg_w_i[0]), row(rg_b_i[0]),
                row(rg_lam[0]), bf(od_w_out[0]), row(norm_mix_post[1]), tiles["rglru_steps"])
    x4 = _ffn(x3.reshape(b * s, d), row(norm_ffn_pre[1]), bf(ffn_w_gate[1]), bf(ffn_w_up[1]),
              bf(ffn_w_down[1]), row(norm_ffn_post[1]), tiles["ffn_rows"])
    return x4.reshape(b, s, d)
```

```python
import functools
import math

import jax
import jax.numpy as jnp
from jax import lax
from jax.experimental import pallas as pl
from jax.experimental.pallas import tpu as pltpu

F32 = jnp.float32
BF16 = jnp.bfloat16

D_MODEL = 1024
HEAD_DIM = 64
ATT_WIDTH = 512
ROPE_DIM = 16
ROPE_THETA = 500000.0
ATT_BLOCK = 128
NEG_INF = -1e30
SSM_WIDTH = 512
SSM_GROUP = 16
SSM_GROUPS = 32
SSM_STATE = 64
LRU_WIDTH = 1024
LRU_BLOCKS = 4
LRU_BLOCK_DIM = 256
CONV_WIDTH = 4
RG_C = 8.0
FFN_HIDDEN = 2816
NORM_EPS = 1e-6

LANES = 128
SUBLANES = 8
N_SLABS = ATT_WIDTH // LANES
GROUPS_PER_SLAB = LANES // SSM_GROUP
SLAB_STATE = GROUPS_PER_SLAB * SSM_STATE
GELU_K1 = 2.0 * math.sqrt(2.0 / math.pi)
GELU_K3 = 8.0 * math.sqrt(2.0 / math.pi) * 0.044715
ATT_SLABS = 2
Q_SCALE = HEAD_DIM ** -0.5 * math.log2(math.e)
VMEM_LIMIT_BYTES = 56 * 1024 * 1024


def _rms_norm(x, g):
    return x * lax.rsqrt(jnp.mean(x * x, axis=-1, keepdims=True) + NORM_EPS) * g


def _const_spec(shape):
    n = len(shape)
    return pl.BlockSpec(shape, lambda *_: (0,) * n, pipeline_mode=pl.Buffered(1))


def _params(*sem):
    return pltpu.CompilerParams(dimension_semantics=sem, vmem_limit_bytes=VMEM_LIMIT_BYTES)


def _even_in_kernel(x_ref, pos_ref, g_ref, w_ref, invf_ref, sgn_ref, q_ref, k_ref, v_ref, u_ref, cos_s, sin_s):
    quarter = pos_ref.shape[1]
    ang = pos_ref[0].astype(F32) * invf_ref[...]
    cos4 = jnp.cos(ang)
    sin4 = jnp.sin(ang) * sgn_ref[...]
    lane128 = lax.broadcasted_iota(jnp.int32, (1, LANES), 1)
    rot_lo = lane128 < ROPE_DIM
    rot_hi = (lane128 >= HEAD_DIM) & (lane128 < HEAD_DIM + ROPE_DIM)
    for r in range(4):
        for packed, dst, ident in ((cos4, cos_s, 1.0), (sin4, sin_s, 0.0)):
            t = packed if r == 0 else pltpu.roll(packed, LANES - 32 * r, 1)
            t = jnp.where(rot_lo, t, ident)
            t = jnp.where(rot_hi, pltpu.roll(t, HEAD_DIM, 1), t)
            dst[pl.ds(r, quarter, stride=4), :] = t
    cos = cos_s[...]
    sin = sin_s[...]
    hn = _rms_norm(x_ref[0], g_ref[...])
    z = jnp.dot(hn.astype(BF16), w_ref[...], preferred_element_type=F32)
    lane = lax.broadcasted_iota(jnp.int32, (1, LANES), 1) % HEAD_DIM
    first_half = lane < (ROPE_DIM // 2)

    def rope(t):
        partner = jnp.where(first_half, pltpu.roll(t, LANES - ROPE_DIM // 2, 1),
                            pltpu.roll(t, ROPE_DIM // 2, 1))
        return t * cos + partner * sin

    for j in range(N_SLABS):
        lo = j * LANES
        q_ref[0, j] = rope(z[:, lo:lo + LANES]) * Q_SCALE
        k_ref[0, j] = rope(z[:, ATT_WIDTH + lo:ATT_WIDTH + lo + LANES])
        v_ref[0, j] = z[:, 2 * ATT_WIDTH + lo:2 * ATT_WIDTH + lo + LANES]
    u_ref[0] = z[:, 3 * ATT_WIDTH:]


def _even_in(x, positions, g, w_in, tm):
    b, s, d = x.shape
    half = ROPE_DIM // 2
    inv_freq = ROPE_THETA ** (-(jnp.arange(half, dtype=F32) * 2.0 / ROPE_DIM))
    lane = jnp.arange(LANES) % 32
    invf = jnp.where(lane < ROPE_DIM, inv_freq[lane % half], 0.0).astype(F32)[None, :]
    sgn = jnp.where(lane < half, -1.0, jnp.where(lane < ROPE_DIM, 1.0, 0.0)).astype(F32)[None, :]
    pos4 = jnp.repeat(positions.reshape(b, s // 4, 4), 32, axis=-1)
    slab = jax.ShapeDtypeStruct((b, N_SLABS, s, LANES), F32)
    slab_spec = pl.BlockSpec((1, N_SLABS, tm, LANES), lambda i, j: (i, 0, j, 0))
    return pl.pallas_call(
        _even_in_kernel,
        grid=(b, s // tm),
        in_specs=[
            pl.BlockSpec((1, tm, d), lambda i, j: (i, j, 0)),
            pl.BlockSpec((1, tm // 4, LANES), lambda i, j: (i, j, 0)),
            _const_spec((1, d)),
            _const_spec(w_in.shape),
            _const_spec((1, LANES)),
            _const_spec((1, LANES)),
        ],
        out_specs=[slab_spec, slab_spec, slab_spec,
                   pl.BlockSpec((1, tm, SSM_WIDTH), lambda i, j: (i, j, 0))],
        out_shape=[slab, slab, slab, jax.ShapeDtypeStruct((b, s, SSM_WIDTH), F32)],
        scratch_shapes=[pltpu.VMEM((tm, LANES), F32)] * 2,
        compiler_params=_params("parallel", "parallel"),
        name="even_in",
    )(x, pos4, g, w_in, invf, sgn)


def _attn_kernel(q_ref, k_ref, v_ref, o_ref, q4, k4, v4, on_ref, lse_ref, *, seq):
    blk = ATT_BLOCK
    sub = seq // 4
    head0 = lax.broadcasted_iota(jnp.int32, (1, LANES), 1) < HEAD_DIM

    def bias(nk):
        qi = lax.broadcasted_iota(jnp.int32, (blk, nk), 0) + (nk - blk)
        ki = lax.broadcasted_iota(jnp.int32, (blk, nk), 1)
        dist = qi - ki
        return jnp.where((dist >= 0) & (dist <= blk), 0.0, NEG_INF).astype(F32)

    biases = {blk: bias(blk), 2 * blk: bias(2 * blk)}

    def unit(qb, kb, vb):
        nk = kb.shape[0]
        k16 = kb.astype(BF16)
        v_ext = jnp.concatenate([vb.astype(BF16), jnp.ones((nk, LANES), BF16)], axis=1)
        outs = []
        for qh in (jnp.where(head0, qb, 0.0), jnp.where(head0, 0.0, qb)):
            s = lax.dot_general(qh.astype(BF16), k16, (((1,), (1,)), ((), ())), preferred_element_type=F32)
            s = s + biases[nk]
            half_max = s if nk == blk else jnp.maximum(s[:, :blk], s[:, blk:])
            m = jnp.max(half_max, axis=-1, keepdims=True)
            p = jnp.exp2(s - m).astype(BF16)
            outs.append((jnp.dot(p, v_ext, preferred_element_type=F32), m))
        (o0, m0), (o1, m1) = outs
        acc = jnp.where(head0, o0[:, :LANES], o1[:, :LANES])
        l = jnp.where(head0, o0[:, LANES:], o1[:, LANES:])
        return acc / l, jnp.where(head0, m0, m1) + jnp.log2(l)

    def head_pair(sl):
        def run_unit(br, src, q_rows, kv_rows, out_rows):
            qs, ks, vs = src
            on, lse = unit(qs(q_rows), ks(kv_rows), vs(kv_rows))
            on_ref[sl, br, out_rows, :] = on
            lse_ref[sl, br, out_rows, :] = lse

        natural = tuple((lambda rows, r=r: r[0, sl, rows, :]) for r in (q_ref, k_ref, v_ref))
        residue_major = tuple((lambda rows, r=r: r[sl, rows, :]) for r in (q4, k4, v4))

        for r in range(4):
            for src, dst in ((q_ref, q4), (k_ref, k4), (v_ref, v4)):
                dst[sl, pl.ds(r * sub, sub), :] = src[0, sl, pl.ds(r, sub, stride=4), :]

        for n in range(seq // blk):
            q_rows = pl.ds(n * blk, blk)
            run_unit(0, natural, q_rows, q_rows if n == 0 else pl.ds((n - 1) * blk, 2 * blk), q_rows)
        for r in range(4):
            for n in range(sub // blk):
                q_rows = pl.ds(r * sub + n * blk, blk)
                run_unit(1, residue_major, q_rows,
                         q_rows if n == 0 else pl.ds(r * sub + (n - 1) * blk, 2 * blk), q_rows)
        assert sub // 4 == blk
        for r16 in range(16):
            rows = pl.ds((r16 % 4) * sub + r16 // 4, blk, stride=4)
            run_unit(2, residue_major, rows, rows, rows)

        for r in range(4):
            nat = pl.ds(r, sub, stride=4)
            rm = pl.ds(r * sub, sub)
            l0, l1, l2 = lse_ref[sl, 0, nat, :], lse_ref[sl, 1, rm, :], lse_ref[sl, 2, rm, :]
            mx = jnp.maximum(jnp.maximum(l0, l1), l2)
            w0, w1, w2 = jnp.exp2(l0 - mx), jnp.exp2(l1 - mx), jnp.exp2(l2 - mx)
            num = w0 * on_ref[sl, 0, nat, :] + w1 * on_ref[sl, 1, rm, :] + w2 * on_ref[sl, 2, rm, :]
            o_ref[0, sl, nat, :] = num / (w0 + w1 + w2)

    for sl in range(ATT_SLABS):
        head_pair(sl)


def _attention(q, k, v):
    b, n_slabs, s, _ = q.shape
    spec = pl.BlockSpec((1, ATT_SLABS, s, LANES), lambda i, j: (i, j, 0, 0))
    return pl.pallas_call(
        functools.partial(_attn_kernel, seq=s),
        grid=(b, n_slabs // ATT_SLABS),
        in_specs=[spec, spec, spec],
        out_specs=spec,
        out_shape=jax.ShapeDtypeStruct((b, n_slabs, s, LANES), F32),
        scratch_shapes=([pltpu.VMEM((ATT_SLABS, s, LANES), F32)] * 3
                        + [pltpu.VMEM((ATT_SLABS, 3, s, LANES), F32)] * 2),
        compiler_params=_params("parallel", "parallel"),
        name="dilated_attention",
    )(q, k, v)


def _s5_kernel(u_ref, bm_ref, are_ref, aim_ref, cm_ref, d_ref, o_ref, utb, hs, hcar, ys, *, ts):
    @pl.when(pl.program_id(2) == 0)
    def _():
        hcar[...] = jnp.zeros_like(hcar)

    for b in range(SUBLANES):
        utb[pl.ds(b, ts, stride=SUBLANES), :] = u_ref[b]
    u_tb = utb[...]
    hs[...] = jnp.dot(u_tb.astype(BF16), bm_ref[0], preferred_element_type=F32)
    ar = jnp.broadcast_to(are_ref[0], (SUBLANES, SLAB_STATE))
    ai = jnp.broadcast_to(aim_ref[0], (SUBLANES, SLAB_STATE))

    def step(t, carry):
        hr, hi = carry
        row = pl.multiple_of(t * SUBLANES, SUBLANES)
        xr = hs[pl.ds(row, SUBLANES), :SLAB_STATE]
        xi = hs[pl.ds(row, SUBLANES), SLAB_STATE:]
        nr = ar * hr - ai * hi + xr
        ni = ar * hi + ai * hr + xi
        hs[pl.ds(row, SUBLANES), :SLAB_STATE] = nr
        hs[pl.ds(row, SUBLANES), SLAB_STATE:] = ni
        return nr, ni

    hr, hi = lax.fori_loop(0, ts, step, (hcar[:, :SLAB_STATE], hcar[:, SLAB_STATE:]), unroll=True)
    hcar[:, :SLAB_STATE] = hr
    hcar[:, SLAB_STATE:] = hi

    y = jnp.dot(hs[...].astype(BF16), cm_ref[0], preferred_element_type=F32)
    ys[...] = jax.nn.gelu(y + d_ref[0] * u_tb)
    for b in range(SUBLANES):
        o_ref[b] = ys[pl.ds(b, ts, stride=SUBLANES), :]


def _s5(u, bmat, a_re, a_im, cmat, dskip, ts):
    b, s, w = u.shape
    rows = SUBLANES * ts
    return pl.pallas_call(
        functools.partial(_s5_kernel, ts=ts),
        grid=(b // SUBLANES, N_SLABS, s // ts),
        in_specs=[
            pl.BlockSpec((SUBLANES, ts, LANES), lambda i, j, t: (i, t, j)),
            pl.BlockSpec((1, LANES, 2 * SLAB_STATE), lambda i, j, t: (j, 0, 0)),
            pl.BlockSpec((1, 1, SLAB_STATE), lambda i, j, t: (j, 0, 0)),
            pl.BlockSpec((1, 1, SLAB_STATE), lambda i, j, t: (j, 0, 0)),
            pl.BlockSpec((1, 2 * SLAB_STATE, LANES), lambda i, j, t: (j, 0, 0)),
            pl.BlockSpec((1, 1, LANES), lambda i, j, t: (j, 0, 0)),
        ],
        out_specs=pl.BlockSpec((SUBLANES, ts, LANES), lambda i, j, t: (i, t, j)),
        out_shape=jax.ShapeDtypeStruct((b, s, w), F32),
        scratch_shapes=[
            pltpu.VMEM((rows, LANES), F32),
            pltpu.VMEM((rows, 2 * SLAB_STATE), F32),
            pltpu.VMEM((SUBLANES, 2 * SLAB_STATE), F32),
            pltpu.VMEM((rows, LANES), F32),
        ],
        compiler_params=_params("parallel", "parallel", "arbitrary"),
        name="s5_scan",
    )(u, bmat, a_re, a_im, cmat, dskip)


def _s5_params(a_re, a_im, log_dt, b_re, b_im, c_re, c_im, d_skip):
    a = lax.complex(a_re.astype(F32), a_im.astype(F32))
    dt = jnp.exp(log_dt.astype(F32))[:, None]
    a_bar = jnp.exp(a * dt)
    b_bar = ((a_bar - 1.0) / a)[..., None] * lax.complex(b_re.astype(F32), b_im.astype(F32))
    eye = jnp.eye(GROUPS_PER_SLAB, dtype=F32)

    def block_diag(m):
        g, r, c = m.shape
        m = m.reshape(N_SLABS, GROUPS_PER_SLAB, r, c)
        return jnp.einsum('sgrc,gh->sgrhc', m, eye).reshape(N_SLABS, GROUPS_PER_SLAB * r, GROUPS_PER_SLAB * c)

    b_t = jnp.swapaxes(b_bar, 1, 2)
    bmat = jnp.concatenate([block_diag(jnp.real(b_t)), block_diag(jnp.imag(b_t))], axis=-1)
    c_re_t = jnp.swapaxes(c_re.astype(F32), 1, 2)
    c_im_t = jnp.swapaxes(c_im.astype(F32), 1, 2)
    cmat = jnp.concatenate([block_diag(c_re_t), block_diag(-c_im_t)], axis=1)
    return (bmat.astype(BF16),
            jnp.real(a_bar).reshape(N_SLABS, 1, SLAB_STATE),
            jnp.imag(a_bar).reshape(N_SLABS, 1, SLAB_STATE),
            cmat.astype(BF16),
            d_skip.astype(F32).reshape(N_SLABS, 1, LANES))


def _ffn_block(x, g_pre, w_gate, w_up, w_down, g_post):
    hn = _rms_norm(x, g_pre).astype(BF16)
    gate = jnp.dot(hn, w_gate, preferred_element_type=F32)
    up = jnp.dot(hn, w_up, preferred_element_type=F32)
    act = (jax.nn.silu(gate) * up).astype(BF16)
    ff = jnp.dot(act, w_down, preferred_element_type=F32)
    return x + _rms_norm(ff, g_post)


def _ffn_kernel(x_ref, gpre_ref, wg_ref, wu_ref, wd_ref, gpost_ref, o_ref):
    o_ref[...] = _ffn_block(x_ref[...], gpre_ref[...], wg_ref[...], wu_ref[...], wd_ref[...], gpost_ref[...])


def _ffn(x, g_pre, w_gate, w_up, w_down, g_post, tm):
    t, d = x.shape
    row_spec = pl.BlockSpec((tm, d), lambda i: (i, 0))
    return pl.pallas_call(
        _ffn_kernel,
        grid=(t // tm,),
        in_specs=[row_spec, _const_spec((1, d)), _const_spec(w_gate.shape), _const_spec(w_up.shape),
                  _const_spec(w_down.shape), _const_spec((1, d))],
        out_specs=row_spec,
        out_shape=jax.ShapeDtypeStruct((t, d), F32),
        compiler_params=_params("parallel"),
        name="ffn",
    )(x, g_pre, w_gate, w_up, w_down, g_post)


def _even_post_kernel(att_ref, yg_ref, x_ref, wglu_ref, wout_ref, gpost_ref,
                      gfpre_ref, wg_ref, wu_ref, wd_ref, gfpost_ref, o_ref):
    yg = yg_ref[...]
    glu = jnp.dot(yg.astype(BF16), wglu_ref[...], preferred_element_type=F32)
    ssm = (yg * jax.nn.sigmoid(glu)).astype(BF16)
    att = jnp.concatenate([att_ref[0, j] for j in range(N_SLABS)], axis=1).astype(BF16)
    mix = (jnp.dot(att, wout_ref[:ATT_WIDTH, :], preferred_element_type=F32)
           + jnp.dot(ssm, wout_ref[ATT_WIDTH:, :], preferred_element_type=F32))
    x1 = x_ref[...] + _rms_norm(mix, gpost_ref[...])
    o_ref[...] = _ffn_block(x1, gfpre_ref[...], wg_ref[...], wu_ref[...], wd_ref[...], gfpost_ref[...])


def _even_post_ffn(att, yg, x, w_glu, w_out, g_post, g_fpre, w_gate, w_up, w_down, g_fpost, tm):
    t, d = x.shape
    tiles_per_seq = att.shape[2] // tm
    row = lambda w: pl.BlockSpec((tm, w), lambda i: (i, 0))
    att_spec = pl.BlockSpec((1, N_SLABS, tm, LANES), lambda i: (i // tiles_per_seq, 0, i % tiles_per_seq, 0))
    return pl.pallas_call(
        _even_post_kernel,
        grid=(t // tm,),
        in_specs=[att_spec, row(SSM_WIDTH), row(d), _const_spec(w_glu.shape), _const_spec(w_out.shape),
                  _const_spec((1, d)), _const_spec((1, d)), _const_spec(w_gate.shape),
                  _const_spec(w_up.shape), _const_spec(w_down.shape), _const_spec((1, d))],
        out_specs=row(d),
        out_shape=jax.ShapeDtypeStruct((t, d), F32),
        compiler_params=_params("parallel"),
        name="even_post_ffn",
    )(att, yg, x, w_glu, w_out, g_post, g_fpre, w_gate, w_up, w_down, g_fpost)


def _rglru_kernel(x_ref, gpre_ref, win_ref, convw_ref, convb_ref, wr_ref, br_ref, wi_ref, bi_ref, lam_ref,
                  wout_ref, gpost_ref, o_ref, hn_tb, xprev, hcar, a_s, b_s, mix_tb, *, ts):
    rows = SUBLANES * ts
    n_slabs = LRU_WIDTH // LANES
    tail = (CONV_WIDTH - 1) * SUBLANES

    @pl.when(pl.program_id(1) == 0)
    def _():
        xprev[...] = jnp.zeros_like(xprev)
        hcar[...] = jnp.zeros_like(hcar)

    for b in range(SUBLANES):
        hn = _rms_norm(x_ref[b], gpre_ref[...])
        for j in range(n_slabs):
            hn_tb[j, pl.ds(b, ts, stride=SUBLANES), :] = hn[:, j * LANES:(j + 1) * LANES]
    hn_rows = jnp.concatenate([hn_tb[j] for j in range(n_slabs)], axis=1).astype(BF16)
    z = jnp.dot(hn_rows, win_ref[...], preferred_element_type=F32)
    xb = z[:, :LRU_WIDTH]
    gate_half = z[:, LRU_WIDTH:]

    xext = jnp.concatenate([xprev[...], xb], axis=0)
    xc_half = convb_ref[...]
    for kk in range(CONV_WIDTH):
        xc_half = xc_half + convw_ref[kk:kk + 1, :] * xext[kk * SUBLANES:kk * SUBLANES + rows]
    xprev[...] = xb[rows - tail:]

    xcb = xc_half.astype(BF16)

    def block_tanh(w_ref, bias_ref):
        parts = [jnp.dot(xcb[:, h * LRU_BLOCK_DIM:(h + 1) * LRU_BLOCK_DIM], w_ref[h],
                         preferred_element_type=F32) for h in range(LRU_BLOCKS)]
        return jnp.tanh(jnp.concatenate(parts, axis=1) + bias_ref[...])

    t_r = block_tanh(wr_ref, br_ref)
    t_i = block_tanh(wi_ref, bi_ref)
    c_half = (-0.5 * RG_C) * jax.nn.softplus(-lam_ref[...])
    log_a = t_r * c_half + c_half
    a = jnp.exp(log_a)
    v = -jnp.tanh(log_a) * (1.0 + a * a)
    mult = jnp.where(v > 0.0, v * lax.rsqrt(v), 0.0)
    a_s[...] = a
    b_s[...] = (t_i + 1.0) * (xc_half * mult)

    def step(t, h):
        row = pl.multiple_of(t * SUBLANES, SUBLANES)
        h = a_s[pl.ds(row, SUBLANES), :] * h + b_s[pl.ds(row, SUBLANES), :]
        b_s[pl.ds(row, SUBLANES), :] = h
        return h

    hcar[...] = lax.fori_loop(0, ts, step, hcar[...], unroll=True)

    inner = gate_half * (GELU_K1 + GELU_K3 * (gate_half * gate_half))
    y = ((b_s[...] * gate_half) * (1.0 + jnp.tanh(inner))).astype(BF16)
    mix = _rms_norm(jnp.dot(y, wout_ref[...], preferred_element_type=F32), gpost_ref[...])
    for j in range(n_slabs):
        mix_tb[j] = mix[:, j * LANES:(j + 1) * LANES]
    for b in range(SUBLANES):
        mix_b = jnp.concatenate([mix_tb[j, pl.ds(b, ts, stride=SUBLANES), :] for j in range(n_slabs)], axis=1)
        o_ref[b] = x_ref[b] + mix_b


def _rglru(x, g_pre, w_in, conv_w, conv_b, w_r, b_r, w_i, b_i, lam, w_out, g_post, ts):
    b, s, d = x.shape
    rows = SUBLANES * ts
    blk = pl.BlockSpec((SUBLANES, ts, d), lambda i, t: (i, t, 0))
    w_in = jnp.concatenate([w_in[:, :LRU_WIDTH], w_in[:, LRU_WIDTH:] * 0.5], axis=1)
    conv_w, conv_b, b_r, b_i = 0.5 * conv_w, 0.5 * conv_b, 0.5 * b_r, 0.5 * b_i
    return pl.pallas_call(
        functools.partial(_rglru_kernel, ts=ts),
        grid=(b // SUBLANES, s // ts),
        in_specs=[blk, _const_spec((1, d)), _const_spec(w_in.shape), _const_spec(conv_w.shape),
                  _const_spec((1, d)), _const_spec(w_r.shape), _const_spec((1, d)), _const_spec(w_i.shape),
                  _const_spec((1, d)), _const_spec((1, d)), _const_spec(w_out.shape), _const_spec((1, d))],
        out_specs=blk,
        out_shape=jax.ShapeDtypeStruct((b, s, d), F32),
        scratch_shapes=[
            pltpu.VMEM((d // LANES, rows, LANES), F32),
            pltpu.VMEM(((CONV_WIDTH - 1) * SUBLANES, LRU_WIDTH), F32),
            pltpu.VMEM((SUBLANES, LRU_WIDTH), F32),
            pltpu.VMEM((rows, LRU_WIDTH), F32),
            pltpu.VMEM((rows, LRU_WIDTH), F32),
            pltpu.VMEM((d // LANES, rows, LANES), F32),
        ],
        compiler_params=_params("parallel", "arbitrary"),
        name="rglru_block",
    )(x, g_pre, w_in, conv_w, conv_b, w_r, b_r, w_i, b_i, lam, w_out, g_post)


def _tiles(s):
    return dict(even_in_rows=min(1024, s), s5_steps=min(512, s), ffn_rows=512, rglru_steps=min(128, s))


def kernel(x, positions, norm_mix_pre, norm_mix_post, norm_ffn_pre, norm_ffn_post, ev_w_in, ev_w_out,
           s5_a_re, s5_a_im, s5_log_dt, s5_b_re, s5_b_im, s5_c_re, s5_c_im, s5_d, s5_w_glu,
           od_w_in, od_w_out, rg_conv_w, rg_conv_b, rg_w_r, rg_b_r, rg_w_i, rg_b_i, rg_lam,
           ffn_w_gate, ffn_w_up, ffn_w_down):
    b, s, d = x.shape
    assert d == D_MODEL and b % SUBLANES == 0 and s % (16 * ATT_BLOCK) == 0
    tiles = _tiles(s)
    row = lambda v: v.astype(F32).reshape(1, -1)
    bf = lambda w: w.astype(BF16)

    q, k, v, u = _even_in(x, positions, row(norm_mix_pre[0]), bf(ev_w_in[0]), tiles["even_in_rows"])
    att = _attention(q, k, v)
    yg = _s5(u, *_s5_params(s5_a_re[0], s5_a_im[0], s5_log_dt[0], s5_b_re[0], s5_b_im[0],
                            s5_c_re[0], s5_c_im[0], s5_d[0]), tiles["s5_steps"])
    x2 = _even_post_ffn(att, yg.reshape(b * s, SSM_WIDTH), x.reshape(b * s, d),
                        bf(s5_w_glu[0]), bf(ev_w_out[0]), row(norm_mix_post[0]),
                        row(norm_ffn_pre[0]), bf(ffn_w_gate[0]), bf(ffn_w_up[0]), bf(ffn_w_down[0]),
                        row(norm_ffn_post[0]), tiles["ffn_rows"])

    x3 = _rglru(x2.reshape(b, s, d), row(norm_mix_pre[1]), bf(od_w_in[0]), rg_conv_w[0].astype(F32),
                row(rg_conv_b[0]), bf(rg_w_r[0]), row(rg_b_r[0]), bf(rg_w_i[0]), row(rg_b_i[0]),
                row(rg_lam[0]), bf(od_w_out[0]), row(norm_mix_post[1]), tiles["rglru_steps"])
    x4 = _ffn(x3.reshape(b * s, d), row(norm_ffn_pre[1]), bf(ffn_w_gate[1]), bf(ffn_w_up[1]),
              bf(ffn_w_down[1]), row(norm_ffn_post[1]), tiles["ffn_rows"])
    return x4.reshape(b, s, d)
```

```python
import functools
import math

import jax
import jax.numpy as jnp
from jax import lax
from jax.experimental import pallas as pl
from jax.experimental.pallas import tpu as pltpu

F32 = jnp.float32
BF16 = jnp.bfloat16

D_MODEL = 1024
HEAD_DIM = 64
ATT_WIDTH = 512
ROPE_DIM = 16
ROPE_THETA = 500000.0
ATT_BLOCK = 128
NEG_INF = -1e30
SSM_WIDTH = 512
SSM_GROUP = 16
SSM_GROUPS = 32
SSM_STATE = 64
LRU_WIDTH = 1024
LRU_BLOCKS = 4
LRU_BLOCK_DIM = 256
CONV_WIDTH = 4
RG_C = 8.0
FFN_HIDDEN = 2816
NORM_EPS = 1e-6

LANES = 128
SUBLANES = 8
N_SLABS = ATT_WIDTH // LANES
GROUPS_PER_SLAB = LANES // SSM_GROUP
SLAB_STATE = GROUPS_PER_SLAB * SSM_STATE
GELU_K1 = 2.0 * math.sqrt(2.0 / math.pi)
GELU_K3 = 8.0 * math.sqrt(2.0 / math.pi) * 0.044715
ATT_SLABS = 2
ATT_SCRATCH_SETS = 2
Q_SCALE = HEAD_DIM ** -0.5 * math.log2(math.e)
VMEM_LIMIT_BYTES = 56 * 1024 * 1024


def _rms_norm(x, g):
    return x * lax.rsqrt(jnp.mean(x * x, axis=-1, keepdims=True) + NORM_EPS) * g


def _const_spec(shape):
    n = len(shape)
    return pl.BlockSpec(shape, lambda *_: (0,) * n, pipeline_mode=pl.Buffered(1))


def _layer_spec(stacked, layer):
    return pl.BlockSpec((None,) + stacked.shape[1:], lambda *_: (layer, 0, 0), pipeline_mode=pl.Buffered(1))


def _params(*sem):
    return pltpu.CompilerParams(dimension_semantics=sem, vmem_limit_bytes=VMEM_LIMIT_BYTES)


def _even_in_kernel(x_ref, pos_ref, g_ref, w_ref, invf_ref, sgn_ref, q_ref, k_ref, v_ref, u_ref, cos_s, sin_s):
    quarter = pos_ref.shape[1]
    ang = pos_ref[0].astype(F32) * invf_ref[...]
    cos4 = jnp.cos(ang)
    sin4 = jnp.sin(ang) * sgn_ref[...]
    lane128 = lax.broadcasted_iota(jnp.int32, (1, LANES), 1)
    rot_lo = lane128 < ROPE_DIM
    rot_hi = (lane128 >= HEAD_DIM) & (lane128 < HEAD_DIM + ROPE_DIM)
    for r in range(4):
        for packed, dst, ident in ((cos4, cos_s, 1.0), (sin4, sin_s, 0.0)):
            t = packed if r == 0 else pltpu.roll(packed, LANES - 32 * r, 1)
            t = jnp.where(rot_lo, t, ident)
            t = jnp.where(rot_hi, pltpu.roll(t, HEAD_DIM, 1), t)
            dst[pl.ds(r, quarter, stride=4), :] = t
    cos = cos_s[...]
    sin = sin_s[...]
    hn = _rms_norm(x_ref[0], g_ref[...])
    z = jnp.dot(hn.astype(BF16), w_ref[...], preferred_element_type=F32)
    lane = lax.broadcasted_iota(jnp.int32, (1, LANES), 1) % HEAD_DIM
    first_half = lane < (ROPE_DIM // 2)

    def rope(t):
        partner = jnp.where(first_half, pltpu.roll(t, LANES - ROPE_DIM // 2, 1),
                            pltpu.roll(t, ROPE_DIM // 2, 1))
        return t * cos + partner * sin

    for j in range(N_SLABS):
        lo = j * LANES
        q_ref[0, j] = rope(z[:, lo:lo + LANES]) * Q_SCALE
        k_ref[0, j] = rope(z[:, ATT_WIDTH + lo:ATT_WIDTH + lo + LANES])
        v_ref[0, j] = z[:, 2 * ATT_WIDTH + lo:2 * ATT_WIDTH + lo + LANES]
    u_ref[0] = z[:, 3 * ATT_WIDTH:]


def _even_in(x, positions, g, w_in, tm):
    b, s, d = x.shape
    half = ROPE_DIM // 2
    inv_freq = ROPE_THETA ** (-(jnp.arange(half, dtype=F32) * 2.0 / ROPE_DIM))
    lane = jnp.arange(LANES) % 32
    invf = jnp.where(lane < ROPE_DIM, inv_freq[lane % half], 0.0).astype(F32)[None, :]
    sgn = jnp.where(lane < half, -1.0, jnp.where(lane < ROPE_DIM, 1.0, 0.0)).astype(F32)[None, :]
    pos4 = jnp.repeat(positions.reshape(b, s // 4, 4), 32, axis=-1)
    slab = jax.ShapeDtypeStruct((b, N_SLABS, s, LANES), F32)
    slab_spec = pl.BlockSpec((1, N_SLABS, tm, LANES), lambda i, j: (i, 0, j, 0))
    return pl.pallas_call(
        _even_in_kernel,
        grid=(b, s // tm),
        in_specs=[
            pl.BlockSpec((1, tm, d), lambda i, j: (i, j, 0)),
            pl.BlockSpec((1, tm // 4, LANES), lambda i, j: (i, j, 0)),
            _const_spec((1, d)),
            _const_spec(w_in.shape),
            _const_spec((1, LANES)),
            _const_spec((1, LANES)),
        ],
        out_specs=[slab_spec, slab_spec, slab_spec,
                   pl.BlockSpec((1, tm, SSM_WIDTH), lambda i, j: (i, j, 0))],
        out_shape=[slab, slab, slab, jax.ShapeDtypeStruct((b, s, SSM_WIDTH), F32)],
        scratch_shapes=[pltpu.VMEM((tm, LANES), F32)] * 2,
        compiler_params=_params("parallel", "parallel"),
        name="even_in",
    )(x, pos4, g, w_in, invf, sgn)


def _attn_kernel(q_ref, k_ref, v_ref, o_ref, q4, k4, v4, on_ref, lse_ref, *, seq):
    blk = ATT_BLOCK
    sub = seq // 4
    head0 = lax.broadcasted_iota(jnp.int32, (1, LANES), 1) < HEAD_DIM

    def bias(nk):
        qi = lax.broadcasted_iota(jnp.int32, (blk, nk), 0) + (nk - blk)
        ki = lax.broadcasted_iota(jnp.int32, (blk, nk), 1)
        dist = qi - ki
        return jnp.where((dist >= 0) & (dist <= blk), 0.0, NEG_INF).astype(F32)

    biases = {blk: bias(blk), 2 * blk: bias(2 * blk)}

    def unit(qb, kb, vb):
        nk = kb.shape[0]
        k16 = kb.astype(BF16)
        v_ext = jnp.concatenate([vb.astype(BF16), jnp.ones((nk, LANES), BF16)], axis=1)
        outs = []
        for qh in (jnp.where(head0, qb, 0.0), jnp.where(head0, 0.0, qb)):
            s = lax.dot_general(qh.astype(BF16), k16, (((1,), (1,)), ((), ())), preferred_element_type=F32)
            s = s + biases[nk]
            half_max = s if nk == blk else jnp.maximum(s[:, :blk], s[:, blk:])
            m = jnp.max(half_max, axis=-1, keepdims=True)
            p = jnp.exp2(s - m).astype(BF16)
            outs.append((jnp.dot(p, v_ext, preferred_element_type=F32), m))
        (o0, m0), (o1, m1) = outs
        acc = jnp.where(head0, o0[:, :LANES], o1[:, :LANES])
        l = jnp.where(head0, o0[:, LANES:], o1[:, LANES:])
        return acc / l, jnp.where(head0, m0, m1) + jnp.log2(l)

    def head_pair(sl):
        ss = sl % ATT_SCRATCH_SETS

        def run_unit(br, src, q_rows, kv_rows, out_rows):
            qs, ks, vs = src
            on, lse = unit(qs(q_rows), ks(kv_rows), vs(kv_rows))
            on_ref[ss, br, out_rows, :] = on
            lse_ref[ss, br, out_rows, :] = lse

        natural = tuple((lambda rows, r=r: r[0, sl, rows, :]) for r in (q_ref, k_ref, v_ref))
        residue_major = tuple((lambda rows, r=r: r[ss, rows, :]) for r in (q4, k4, v4))

        for r in range(4):
            for src, dst in ((q_ref, q4), (k_ref, k4), (v_ref, v4)):
                dst[ss, pl.ds(r * sub, sub), :] = src[0, sl, pl.ds(r, sub, stride=4), :]

        for n in range(seq // blk):
            q_rows = pl.ds(n * blk, blk)
            run_unit(0, natural, q_rows, q_rows if n == 0 else pl.ds((n - 1) * blk, 2 * blk), q_rows)
        for r in range(4):
            for n in range(sub // blk):
                q_rows = pl.ds(r * sub + n * blk, blk)
                run_unit(1, residue_major, q_rows,
                         q_rows if n == 0 else pl.ds(r * sub + (n - 1) * blk, 2 * blk), q_rows)
        assert sub // 4 == blk
        for r16 in range(16):
            rows = pl.ds((r16 % 4) * sub + r16 // 4, blk, stride=4)
            run_unit(2, residue_major, rows, rows, rows)

        for r in range(4):
            nat = pl.ds(r, sub, stride=4)
            rm = pl.ds(r * sub, sub)
            l0, l1, l2 = lse_ref[ss, 0, nat, :], lse_ref[ss, 1, rm, :], lse_ref[ss, 2, rm, :]
            mx = jnp.maximum(jnp.maximum(l0, l1), l2)
            w0, w1, w2 = jnp.exp2(l0 - mx), jnp.exp2(l1 - mx), jnp.exp2(l2 - mx)
            num = w0 * on_ref[ss, 0, nat, :] + w1 * on_ref[ss, 1, rm, :] + w2 * on_ref[ss, 2, rm, :]
            o_ref[0, sl, nat, :] = num / (w0 + w1 + w2)

    for sl in range(ATT_SLABS):
        head_pair(sl)


def _attention(q, k, v):
    b, n_slabs, s, _ = q.shape
    spec = pl.BlockSpec((1, ATT_SLABS, s, LANES), lambda i, j: (i, j, 0, 0))
    return pl.pallas_call(
        functools.partial(_attn_kernel, seq=s),
        grid=(b, n_slabs // ATT_SLABS),
        in_specs=[spec, spec, spec],
        out_specs=spec,
        out_shape=jax.ShapeDtypeStruct((b, n_slabs, s, LANES), F32),
        scratch_shapes=([pltpu.VMEM((ATT_SCRATCH_SETS, s, LANES), F32)] * 3
                        + [pltpu.VMEM((ATT_SCRATCH_SETS, 3, s, LANES), F32)] * 2),
        compiler_params=_params("parallel", "parallel"),
        name="dilated_attention",
    )(q, k, v)


def _s5_kernel(u_ref, bm_ref, are_ref, aim_ref, cm_ref, d_ref, o_ref, utb, hs, hcar, ys, *, ts):
    @pl.when(pl.program_id(2) == 0)
    def _():
        hcar[...] = jnp.zeros_like(hcar)

    for b in range(SUBLANES):
        utb[pl.ds(b, ts, stride=SUBLANES), :] = u_ref[b]
    u_tb = utb[...]
    hs[...] = jnp.dot(u_tb.astype(BF16), bm_ref[0], preferred_element_type=F32)
    ar = jnp.broadcast_to(are_ref[0], (SUBLANES, SLAB_STATE))
    ai = jnp.broadcast_to(aim_ref[0], (SUBLANES, SLAB_STATE))

    def step(t, carry):
        hr, hi = carry
        row = pl.multiple_of(t * SUBLANES, SUBLANES)
        xr = hs[pl.ds(row, SUBLANES), :SLAB_STATE]
        xi = hs[pl.ds(row, SUBLANES), SLAB_STATE:]
        nr = ar * hr - ai * hi + xr
        ni = ar * hi + ai * hr + xi
        hs[pl.ds(row, SUBLANES), :SLAB_STATE] = nr
        hs[pl.ds(row, SUBLANES), SLAB_STATE:] = ni
        return nr, ni

    hr, hi = lax.fori_loop(0, ts, step, (hcar[:, :SLAB_STATE], hcar[:, SLAB_STATE:]), unroll=True)
    hcar[:, :SLAB_STATE] = hr
    hcar[:, SLAB_STATE:] = hi

    y = jnp.dot(hs[...].astype(BF16), cm_ref[0], preferred_element_type=F32)
    ys[...] = jax.nn.gelu(y + d_ref[0] * u_tb)
    for b in range(SUBLANES):
        o_ref[b] = ys[pl.ds(b, ts, stride=SUBLANES), :]


def _s5(u, bmat, a_re, a_im, cmat, dskip, ts):
    b, s, w = u.shape
    rows = SUBLANES * ts
    return pl.pallas_call(
        functools.partial(_s5_kernel, ts=ts),
        grid=(b // SUBLANES, N_SLABS, s // ts),
        in_specs=[
            pl.BlockSpec((SUBLANES, ts, LANES), lambda i, j, t: (i, t, j)),
            pl.BlockSpec((1, LANES, 2 * SLAB_STATE), lambda i, j, t: (j, 0, 0)),
            pl.BlockSpec((1, 1, SLAB_STATE), lambda i, j, t: (j, 0, 0)),
            pl.BlockSpec((1, 1, SLAB_STATE), lambda i, j, t: (j, 0, 0)),
            pl.BlockSpec((1, 2 * SLAB_STATE, LANES), lambda i, j, t: (j, 0, 0)),
            pl.BlockSpec((1, 1, LANES), lambda i, j, t: (j, 0, 0)),
        ],
        out_specs=pl.BlockSpec((SUBLANES, ts, LANES), lambda i, j, t: (i, t, j)),
        out_shape=jax.ShapeDtypeStruct((b, s, w), F32),
        scratch_shapes=[
            pltpu.VMEM((rows, LANES), F32),
            pltpu.VMEM((rows, 2 * SLAB_STATE), F32),
            pltpu.VMEM((SUBLANES, 2 * SLAB_STATE), F32),
            pltpu.VMEM((rows, LANES), F32),
        ],
        compiler_params=_params("parallel", "parallel", "arbitrary"),
        name="s5_scan",
    )(u, bmat, a_re, a_im, cmat, dskip)


def _s5_params(a_re, a_im, log_dt, b_re, b_im, c_re, c_im, d_skip):
    a = lax.complex(a_re.astype(F32), a_im.astype(F32))
    dt = jnp.exp(log_dt.astype(F32))[:, None]
    a_bar = jnp.exp(a * dt)
    b_bar = ((a_bar - 1.0) / a)[..., None] * lax.complex(b_re.astype(F32), b_im.astype(F32))
    eye = jnp.eye(GROUPS_PER_SLAB, dtype=F32)

    def block_diag(m):
        g, r, c = m.shape
        m = m.reshape(N_SLABS, GROUPS_PER_SLAB, r, c)
        return jnp.einsum('sgrc,gh->sgrhc', m, eye).reshape(N_SLABS, GROUPS_PER_SLAB * r, GROUPS_PER_SLAB * c)

    b_t = jnp.swapaxes(b_bar, 1, 2)
    bmat = jnp.concatenate([block_diag(jnp.real(b_t)), block_diag(jnp.imag(b_t))], axis=-1)
    c_re_t = jnp.swapaxes(c_re.astype(F32), 1, 2)
    c_im_t = jnp.swapaxes(c_im.astype(F32), 1, 2)
    cmat = jnp.concatenate([block_diag(c_re_t), block_diag(-c_im_t)], axis=1)
    return (bmat.astype(BF16),
            jnp.real(a_bar).reshape(N_SLABS, 1, SLAB_STATE),
            jnp.imag(a_bar).reshape(N_SLABS, 1, SLAB_STATE),
            cmat.astype(BF16),
            d_skip.astype(F32).reshape(N_SLABS, 1, LANES))


def _ffn_block(x, g_pre, w_gate, w_up, w_down, g_post):
    hn = _rms_norm(x, g_pre).astype(BF16)
    gate = jnp.dot(hn, w_gate, preferred_element_type=F32)
    up = jnp.dot(hn, w_up, preferred_element_type=F32)
    act = (jax.nn.silu(gate) * up).astype(BF16)
    ff = jnp.dot(act, w_down, preferred_element_type=F32)
    return x + _rms_norm(ff, g_post)


def _ffn_kernel(x_ref, gpre_ref, wg_ref, wu_ref, wd_ref, gpost_ref, o_ref):
    o_ref[...] = _ffn_block(x_ref[...], gpre_ref[...], wg_ref[...], wu_ref[...], wd_ref[...], gpost_ref[...])


def _ffn(x, g_pre, w_gate, w_up, w_down, g_post, layer, tm):
    t, d = x.shape
    row_spec = pl.BlockSpec((tm, d), lambda i: (i, 0))
    return pl.pallas_call(
        _ffn_kernel,
        grid=(t // tm,),
        in_specs=[row_spec, _const_spec((1, d)), _layer_spec(w_gate, layer), _layer_spec(w_up, layer),
                  _layer_spec(w_down, layer), _const_spec((1, d))],
        out_specs=row_spec,
        out_shape=jax.ShapeDtypeStruct((t, d), F32),
        compiler_params=_params("parallel"),
        name="ffn",
    )(x, g_pre, w_gate, w_up, w_down, g_post)


def _even_post_kernel(att_ref, yg_ref, x_ref, wglu_ref, wout_ref, gpost_ref,
                      gfpre_ref, wg_ref, wu_ref, wd_ref, gfpost_ref, o_ref):
    yg = yg_ref[...]
    glu = jnp.dot(yg.astype(BF16), wglu_ref[...], preferred_element_type=F32)
    ssm = (yg * jax.nn.sigmoid(glu)).astype(BF16)
    att = jnp.concatenate([att_ref[0, j] for j in range(N_SLABS)], axis=1).astype(BF16)
    mix = (jnp.dot(att, wout_ref[:ATT_WIDTH, :], preferred_element_type=F32)
           + jnp.dot(ssm, wout_ref[ATT_WIDTH:, :], preferred_element_type=F32))
    x1 = x_ref[...] + _rms_norm(mix, gpost_ref[...])
    o_ref[...] = _ffn_block(x1, gfpre_ref[...], wg_ref[...], wu_ref[...], wd_ref[...], gfpost_ref[...])


def _even_post_ffn(att, yg, x, w_glu, w_out, g_post, g_fpre, w_gate, w_up, w_down, g_fpost, layer, tm):
    t, d = x.shape
    tiles_per_seq = att.shape[2] // tm
    row = lambda w: pl.BlockSpec((tm, w), lambda i: (i, 0))
    att_spec = pl.BlockSpec((1, N_SLABS, tm, LANES), lambda i: (i // tiles_per_seq, 0, i % tiles_per_seq, 0))
    return pl.pallas_call(
        _even_post_kernel,
        grid=(t // tm,),
        in_specs=[att_spec, row(SSM_WIDTH), row(d), _const_spec(w_glu.shape), _const_spec(w_out.shape),
                  _const_spec((1, d)), _const_spec((1, d)), _layer_spec(w_gate, layer),
                  _layer_spec(w_up, layer), _layer_spec(w_down, layer), _const_spec((1, d))],
        out_specs=row(d),
        out_shape=jax.ShapeDtypeStruct((t, d), F32),
        compiler_params=_params("parallel"),
        name="even_post_ffn",
    )(att, yg, x, w_glu, w_out, g_post, g_fpre, w_gate, w_up, w_down, g_fpost)


def _rglru_kernel(x_ref, gpre_ref, win_ref, convw_ref, convb_ref, wr_ref, br_ref, wi_ref, bi_ref, lam_ref,
                  wout_ref, gpost_ref, o_ref, hn_tb, xprev, hcar, a_s, b_s, mix_tb, *, ts):
    rows = SUBLANES * ts
    n_slabs = LRU_WIDTH // LANES
    tail = (CONV_WIDTH - 1) * SUBLANES

    @pl.when(pl.program_id(1) == 0)
    def _():
        xprev[...] = jnp.zeros_like(xprev)
        hcar[...] = jnp.zeros_like(hcar)

    for b in range(SUBLANES):
        hn = _rms_norm(x_ref[b], gpre_ref[...])
        for j in range(n_slabs):
            hn_tb[j, pl.ds(b, ts, stride=SUBLANES), :] = hn[:, j * LANES:(j + 1) * LANES]
    hn_rows = jnp.concatenate([hn_tb[j] for j in range(n_slabs)], axis=1).astype(BF16)
    z = jnp.dot(hn_rows, win_ref[...], preferred_element_type=F32)
    xb = z[:, :LRU_WIDTH]
    gate_half = z[:, LRU_WIDTH:]

    xext = jnp.concatenate([xprev[...], xb], axis=0)
    xc_half = convb_ref[...]
    for kk in range(CONV_WIDTH):
        xc_half = xc_half + convw_ref[kk:kk + 1, :] * xext[kk * SUBLANES:kk * SUBLANES + rows]
    xprev[...] = xb[rows - tail:]

    xcb = xc_half.astype(BF16)

    def block_tanh(w_ref, bias_ref):
        parts = [jnp.dot(xcb[:, h * LRU_BLOCK_DIM:(h + 1) * LRU_BLOCK_DIM], w_ref[h],
                         preferred_element_type=F32) for h in range(LRU_BLOCKS)]
        return jnp.tanh(jnp.concatenate(parts, axis=1) + bias_ref[...])

    t_r = block_tanh(wr_ref, br_ref)
    t_i = block_tanh(wi_ref, bi_ref)
    c_half = (-0.5 * RG_C) * jax.nn.softplus(-lam_ref[...])
    log_a = t_r * c_half + c_half
    a = jnp.exp(log_a)
    v = -jnp.tanh(log_a) * (1.0 + a * a)
    mult = jnp.where(v > 0.0, v * lax.rsqrt(v), 0.0)
    a_s[...] = a
    b_s[...] = (t_i + 1.0) * (xc_half * mult)

    def step(t, h):
        row = pl.multiple_of(t * SUBLANES, SUBLANES)
        h = a_s[pl.ds(row, SUBLANES), :] * h + b_s[pl.ds(row, SUBLANES), :]
        b_s[pl.ds(row, SUBLANES), :] = h
        return h

    hcar[...] = lax.fori_loop(0, ts, step, hcar[...], unroll=True)

    inner = gate_half * (GELU_K1 + GELU_K3 * (gate_half * gate_half))
    y = ((b_s[...] * gate_half) * (1.0 + jnp.tanh(inner))).astype(BF16)
    mix = _rms_norm(jnp.dot(y, wout_ref[...], preferred_element_type=F32), gpost_ref[...])
    for j in range(n_slabs):
        mix_tb[j] = mix[:, j * LANES:(j + 1) * LANES]
    for b in range(SUBLANES):
        mix_b = jnp.concatenate([mix_tb[j, pl.ds(b, ts, stride=SUBLANES), :] for j in range(n_slabs)], axis=1)
        o_ref[b] = x_ref[b] + mix_b


def _rglru(x, g_pre, w_in, conv_w, conv_b, w_r, b_r, w_i, b_i, lam, w_out, g_post, ts):
    b, s, d = x.shape
    rows = SUBLANES * ts
    blk = pl.BlockSpec((SUBLANES, ts, d), lambda i, t: (i, t, 0))
    col_scale = jnp.where(jnp.arange(2 * LRU_WIDTH) < LRU_WIDTH, 1.0, 0.5).astype(F32)
    w_in = (w_in * col_scale).astype(BF16)
    conv_w, conv_b, b_r, b_i = 0.5 * conv_w, 0.5 * conv_b, 0.5 * b_r, 0.5 * b_i
    return pl.pallas_call(
        functools.partial(_rglru_kernel, ts=ts),
        grid=(b // SUBLANES, s // ts),
        in_specs=[blk, _const_spec((1, d)), _const_spec(w_in.shape), _const_spec(conv_w.shape),
                  _const_spec((1, d)), _const_spec(w_r.shape), _const_spec((1, d)), _const_spec(w_i.shape),
                  _const_spec((1, d)), _const_spec((1, d)), _const_spec(w_out.shape), _const_spec((1, d))],
        out_specs=blk,
        out_shape=jax.ShapeDtypeStruct((b, s, d), F32),
        scratch_shapes=[
            pltpu.VMEM((d // LANES, rows, LANES), F32),
            pltpu.VMEM(((CONV_WIDTH - 1) * SUBLANES, LRU_WIDTH), F32),
            pltpu.VMEM((SUBLANES, LRU_WIDTH), F32),
            pltpu.VMEM((rows, LRU_WIDTH), F32),
            pltpu.VMEM((rows, LRU_WIDTH), F32),
            pltpu.VMEM((d // LANES, rows, LANES), F32),
        ],
        compiler_params=_params("parallel", "arbitrary"),
        name="rglru_block",
    )(x, g_pre, w_in, conv_w, conv_b, w_r, b_r, w_i, b_i, lam, w_out, g_post)


def _tiles(s):
    return dict(even_in_rows=min(1024, s), s5_steps=min(512, s), ffn_rows=512, rglru_steps=min(128, s))


def kernel(x, positions, norm_mix_pre, norm_mix_post, norm_ffn_pre, norm_ffn_post, ev_w_in, ev_w_out,
           s5_a_re, s5_a_im, s5_log_dt, s5_b_re, s5_b_im, s5_c_re, s5_c_im, s5_d, s5_w_glu,
           od_w_in, od_w_out, rg_conv_w, rg_conv_b, rg_w_r, rg_b_r, rg_w_i, rg_b_i, rg_lam,
           ffn_w_gate, ffn_w_up, ffn_w_down):
    b, s, d = x.shape
    assert d == D_MODEL and b % SUBLANES == 0 and s % (16 * ATT_BLOCK) == 0
    tiles = _tiles(s)
    row = lambda v: v.astype(F32).reshape(1, -1)
    bf = lambda w: w.astype(BF16)
    w_gate, w_up, w_down = bf(ffn_w_gate), bf(ffn_w_up), bf(ffn_w_down)

    q, k, v, u = _even_in(x, positions, row(norm_mix_pre[0]), bf(ev_w_in[0]), tiles["even_in_rows"])
    att = _attention(q, k, v)
    yg = _s5(u, *_s5_params(s5_a_re[0], s5_a_im[0], s5_log_dt[0], s5_b_re[0], s5_b_im[0],
                            s5_c_re[0], s5_c_im[0], s5_d[0]), tiles["s5_steps"])
    x2 = _even_post_ffn(att, yg.reshape(b * s, SSM_WIDTH), x.reshape(b * s, d),
                        bf(s5_w_glu[0]), bf(ev_w_out[0]), row(norm_mix_post[0]),
                        row(norm_ffn_pre[0]), w_gate, w_up, w_down, row(norm_ffn_post[0]), 0, tiles["ffn_rows"])

    x3 = _rglru(x2.reshape(b, s, d), row(norm_mix_pre[1]), od_w_in[0], rg_conv_w[0].astype(F32),
                row(rg_conv_b[0]), bf(rg_w_r[0]), row(rg_b_r[0]), bf(rg_w_i[0]), row(rg_b_i[0]),
                row(rg_lam[0]), bf(od_w_out[0]), row(norm_mix_post[1]), tiles["rglru_steps"])
    x4 = _ffn(x3.reshape(b * s, d), row(norm_ffn_pre[1]), w_gate, w_up, w_down, row(norm_ffn_post[1]), 1,
              tiles["ffn_rows"])
    return x4.reshape(b, s, d)
```

```python
import functools
import math

import jax
import jax.numpy as jnp
from jax import lax
from jax.experimental import pallas as pl
from jax.experimental.pallas import tpu as pltpu

F32 = jnp.float32
BF16 = jnp.bfloat16

D_MODEL = 1024
HEAD_DIM = 64
ATT_WIDTH = 512
ROPE_DIM = 16
ROPE_THETA = 500000.0
ATT_BLOCK = 128
NEG_INF = -1e30
SSM_WIDTH = 512
SSM_GROUP = 16
SSM_GROUPS = 32
SSM_STATE = 64
LRU_WIDTH = 1024
LRU_BLOCKS = 4
LRU_BLOCK_DIM = 256
CONV_WIDTH = 4
RG_C = 8.0
FFN_HIDDEN = 2816
NORM_EPS = 1e-6

LANES = 128
SUBLANES = 8
N_SLABS = ATT_WIDTH // LANES
GROUPS_PER_SLAB = LANES // SSM_GROUP
SLAB_STATE = GROUPS_PER_SLAB * SSM_STATE
GELU_K1 = 2.0 * math.sqrt(2.0 / math.pi)
GELU_K3 = 8.0 * math.sqrt(2.0 / math.pi) * 0.044715
ATT_SLABS = 2
DIL_MID = 4
ROT_PACK = 4
ROT_PACK_LANES = LANES // ROT_PACK
Q_SCALE = HEAD_DIM ** -0.5 * math.log2(math.e)
VMEM_LIMIT_BYTES = 56 * 1024 * 1024


def _rms_norm(x, g):
    return x * lax.rsqrt(jnp.mean(x * x, axis=-1, keepdims=True) + NORM_EPS) * g


def _const_spec(shape):
    n = len(shape)
    return pl.BlockSpec(shape, lambda *_: (0,) * n, pipeline_mode=pl.Buffered(1))


def _layer_spec(stacked, layer):
    return pl.BlockSpec((None,) + stacked.shape[1:], lambda *_: (layer, 0, 0), pipeline_mode=pl.Buffered(1))


def _params(*sem):
    return pltpu.CompilerParams(dimension_semantics=sem, vmem_limit_bytes=VMEM_LIMIT_BYTES)


def _even_in_kernel(x_ref, pos_ref, g_ref, w_ref, invf_ref, sgn_ref, q_ref, k_ref, v_ref, u_ref, cos_s, sin_s):
    packed_rows = pos_ref.shape[1]
    ang = pos_ref[0].astype(F32) * invf_ref[...]
    cos4 = jnp.cos(ang)
    sin4 = jnp.sin(ang) * sgn_ref[...]
    lane128 = lax.broadcasted_iota(jnp.int32, (1, LANES), 1)
    rot_lo = lane128 < ROPE_DIM
    rot_hi = (lane128 >= HEAD_DIM) & (lane128 < HEAD_DIM + ROPE_DIM)
    for r in range(ROT_PACK):
        for packed, dst, ident in ((cos4, cos_s, 1.0), (sin4, sin_s, 0.0)):
            t = packed if r == 0 else pltpu.roll(packed, LANES - ROT_PACK_LANES * r, 1)
            t = jnp.where(rot_lo, t, ident)
            t = jnp.where(rot_hi, pltpu.roll(t, HEAD_DIM, 1), t)
            dst[pl.ds(r, packed_rows, stride=ROT_PACK), :] = t
    cos = cos_s[...]
    sin = sin_s[...]
    hn = _rms_norm(x_ref[0], g_ref[...])
    z = jnp.dot(hn.astype(BF16), w_ref[...], preferred_element_type=F32)
    lane = lax.broadcasted_iota(jnp.int32, (1, LANES), 1) % HEAD_DIM
    first_half = lane < (ROPE_DIM // 2)

    def rope(t):
        partner = jnp.where(first_half, pltpu.roll(t, LANES - ROPE_DIM // 2, 1),
                            pltpu.roll(t, ROPE_DIM // 2, 1))
        return t * cos + partner * sin

    for j in range(N_SLABS):
        lo = j * LANES
        q_ref[0, j] = rope(z[:, lo:lo + LANES]) * Q_SCALE
        k_ref[0, j] = rope(z[:, ATT_WIDTH + lo:ATT_WIDTH + lo + LANES])
        v_ref[0, j] = z[:, 2 * ATT_WIDTH + lo:2 * ATT_WIDTH + lo + LANES]
    u_ref[0] = z[:, 3 * ATT_WIDTH:]


def _even_in(x, positions, g, w_in, tm):
    b, s, d = x.shape
    half = ROPE_DIM // 2
    inv_freq = ROPE_THETA ** (-(jnp.arange(half, dtype=F32) * 2.0 / ROPE_DIM))
    assert ROPE_DIM <= ROT_PACK_LANES and 2 * HEAD_DIM == LANES and tm % (ROT_PACK * SUBLANES) == 0
    lane = jnp.arange(LANES) % ROT_PACK_LANES
    invf = jnp.where(lane < ROPE_DIM, inv_freq[lane % half], 0.0).astype(F32)[None, :]
    sgn = jnp.where(lane < half, -1.0, jnp.where(lane < ROPE_DIM, 1.0, 0.0)).astype(F32)[None, :]
    pos4 = jnp.repeat(positions.reshape(b, s // ROT_PACK, ROT_PACK), ROT_PACK_LANES, axis=-1)
    slab = jax.ShapeDtypeStruct((b, N_SLABS, s, LANES), F32)
    slab_spec = pl.BlockSpec((1, N_SLABS, tm, LANES), lambda i, j: (i, 0, j, 0))
    return pl.pallas_call(
        _even_in_kernel,
        grid=(b, s // tm),
        in_specs=[
            pl.BlockSpec((1, tm, d), lambda i, j: (i, j, 0)),
            pl.BlockSpec((1, tm // ROT_PACK, LANES), lambda i, j: (i, j, 0)),
            _const_spec((1, d)),
            _const_spec(w_in.shape),
            _const_spec((1, LANES)),
            _const_spec((1, LANES)),
        ],
        out_specs=[slab_spec, slab_spec, slab_spec,
                   pl.BlockSpec((1, tm, SSM_WIDTH), lambda i, j: (i, j, 0))],
        out_shape=[slab, slab, slab, jax.ShapeDtypeStruct((b, s, SSM_WIDTH), F32)],
        scratch_shapes=[pltpu.VMEM((tm, LANES), F32)] * 2,
        compiler_params=_params("parallel", "parallel"),
        name="even_in",
    )(x, pos4, g, w_in, invf, sgn)


def _attn_kernel(q_ref, k_ref, v_ref, o_ref, q4, k4, v4, on_ref, lse_ref, *, seq):
    blk = ATT_BLOCK
    sub = seq // DIL_MID
    head0 = lax.broadcasted_iota(jnp.int32, (1, LANES), 1) < HEAD_DIM

    def bias(nk):
        ki = lax.broadcasted_iota(jnp.int32, (nk, blk), 0)
        qi = lax.broadcasted_iota(jnp.int32, (nk, blk), 1) + (nk - blk)
        dist = qi - ki
        return jnp.where((dist >= 0) & (dist <= blk), 0.0, NEG_INF).astype(F32)

    biases = {blk: bias(blk), 2 * blk: bias(2 * blk)}
    dim_head0 = lax.broadcasted_iota(jnp.int32, (LANES, 1), 0) < HEAD_DIM

    def unit(qb, kb, vb):
        nk = kb.shape[0]
        k16 = kb.astype(BF16)
        vt16 = vb.T.astype(BF16)
        q2 = jnp.concatenate([jnp.where(head0, qb, 0.0), jnp.where(head0, 0.0, qb)], axis=0).astype(BF16)
        st2 = lax.dot_general(k16, q2, (((1,), (1,)), ((), ())), preferred_element_type=F32)
        ps, stats = [], []
        for h in range(2):
            st = st2[:, h * blk:(h + 1) * blk] + biases[nk]
            m = jnp.max(st, axis=0, keepdims=True)
            p = jnp.exp2(st - m)
            l = jnp.sum(p, axis=0, keepdims=True)
            ps.append(p.astype(BF16))
            stats.append((1.0 / l, m + jnp.log2(l)))
        ot2 = jnp.dot(vt16, jnp.concatenate(ps, axis=1), preferred_element_type=F32)
        on_t = jnp.where(dim_head0, ot2[:, :blk] * stats[0][0], ot2[:, blk:] * stats[1][0])
        lse_t = jnp.where(dim_head0, stats[0][1], stats[1][1])
        return on_t.T, lse_t.T

    def head_pair(sl):
        def run_unit(br, src, q_rows, kv_rows, out_rows):
            qs, ks, vs = src
            on, lse = unit(qs(q_rows), ks(kv_rows), vs(kv_rows))
            on_ref[sl, br, out_rows, :] = on
            lse_ref[sl, br, out_rows, :] = lse

        natural = tuple((lambda rows, r=r: r[0, sl, rows, :]) for r in (q_ref, k_ref, v_ref))
        residue_major = tuple((lambda rows, r=r: r[sl, rows, :]) for r in (q4, k4, v4))

        for r in range(DIL_MID):
            for src, dst in ((q_ref, q4), (k_ref, k4), (v_ref, v4)):
                dst[sl, pl.ds(r * sub, sub), :] = src[0, sl, pl.ds(r, sub, stride=DIL_MID), :]

        for n in range(seq // blk):
            q_rows = pl.ds(n * blk, blk)
            run_unit(0, natural, q_rows, q_rows if n == 0 else pl.ds((n - 1) * blk, 2 * blk), q_rows)
        for r in range(DIL_MID):
            for n in range(sub // blk):
                q_rows = pl.ds(r * sub + n * blk, blk)
                run_unit(1, residue_major, q_rows,
                         q_rows if n == 0 else pl.ds(r * sub + (n - 1) * blk, 2 * blk), q_rows)
        assert sub // DIL_MID == blk
        for far in range(DIL_MID * DIL_MID):
            rows = pl.ds((far % DIL_MID) * sub + far // DIL_MID, blk, stride=DIL_MID)
            run_unit(2, residue_major, rows, rows, rows)

        for r in range(DIL_MID):
            nat = pl.ds(r, sub, stride=DIL_MID)
            rm = pl.ds(r * sub, sub)
            l0, l1, l2 = lse_ref[sl, 0, nat, :], lse_ref[sl, 1, rm, :], lse_ref[sl, 2, rm, :]
            mx = jnp.maximum(jnp.maximum(l0, l1), l2)
            w0, w1, w2 = jnp.exp2(l0 - mx), jnp.exp2(l1 - mx), jnp.exp2(l2 - mx)
            num = w0 * on_ref[sl, 0, nat, :] + w1 * on_ref[sl, 1, rm, :] + w2 * on_ref[sl, 2, rm, :]
            o_ref[0, sl, nat, :] = num / (w0 + w1 + w2)

    for sl in range(ATT_SLABS):
        head_pair(sl)


def _attention(q, k, v):
    b, n_slabs, s, _ = q.shape
    spec = pl.BlockSpec((1, ATT_SLABS, s, LANES), lambda i, j: (i, j, 0, 0))
    return pl.pallas_call(
        functools.partial(_attn_kernel, seq=s),
        grid=(b, n_slabs // ATT_SLABS),
        in_specs=[spec, spec, spec],
        out_specs=spec,
        out_shape=jax.ShapeDtypeStruct((b, n_slabs, s, LANES), F32),
        scratch_shapes=([pltpu.VMEM((ATT_SLABS, s, LANES), F32)] * 3
                        + [pltpu.VMEM((ATT_SLABS, 3, s, LANES), F32)] * 2),
        compiler_params=_params("parallel", "parallel"),
        name="dilated_attention",
    )(q, k, v)


def _s5_kernel(u_ref, bm_ref, are_ref, aim_ref, cm_ref, d_ref, o_ref, utb, hs, hcar, ys, *, ts):
    @pl.when(pl.program_id(2) == 0)
    def _():
        hcar[...] = jnp.zeros_like(hcar)

    for b in range(SUBLANES):
        utb[pl.ds(b, ts, stride=SUBLANES), :] = u_ref[b]
    u_tb = utb[...]
    hs[...] = jnp.dot(u_tb.astype(BF16), bm_ref[0], preferred_element_type=F32)
    ar = jnp.broadcast_to(are_ref[0], (SUBLANES, SLAB_STATE))
    ai = jnp.broadcast_to(aim_ref[0], (SUBLANES, SLAB_STATE))

    def step(t, carry):
        hr, hi = carry
        row = pl.multiple_of(t * SUBLANES, SUBLANES)
        xr = hs[pl.ds(row, SUBLANES), :SLAB_STATE]
        xi = hs[pl.ds(row, SUBLANES), SLAB_STATE:]
        nr = ar * hr - ai * hi + xr
        ni = ar * hi + ai * hr + xi
        hs[pl.ds(row, SUBLANES), :SLAB_STATE] = nr
        hs[pl.ds(row, SUBLANES), SLAB_STATE:] = ni
        return nr, ni

    hr, hi = lax.fori_loop(0, ts, step, (hcar[:, :SLAB_STATE], hcar[:, SLAB_STATE:]), unroll=True)
    hcar[:, :SLAB_STATE] = hr
    hcar[:, SLAB_STATE:] = hi

    y = jnp.dot(hs[...].astype(BF16), cm_ref[0], preferred_element_type=F32)
    ys[...] = jax.nn.gelu(y + d_ref[0] * u_tb)
    for b in range(SUBLANES):
        o_ref[b] = ys[pl.ds(b, ts, stride=SUBLANES), :]


def _s5(u, bmat, a_re, a_im, cmat, dskip, ts):
    b, s, w = u.shape
    rows = SUBLANES * ts
    return pl.pallas_call(
        functools.partial(_s5_kernel, ts=ts),
        grid=(b // SUBLANES, N_SLABS, s // ts),
        in_specs=[
            pl.BlockSpec((SUBLANES, ts, LANES), lambda i, j, t: (i, t, j)),
            pl.BlockSpec((1, LANES, 2 * SLAB_STATE), lambda i, j, t: (j, 0, 0)),
            pl.BlockSpec((1, 1, SLAB_STATE), lambda i, j, t: (j, 0, 0)),
            pl.BlockSpec((1, 1, SLAB_STATE), lambda i, j, t: (j, 0, 0)),
            pl.BlockSpec((1, 2 * SLAB_STATE, LANES), lambda i, j, t: (j, 0, 0)),
            pl.BlockSpec((1, 1, LANES), lambda i, j, t: (j, 0, 0)),
        ],
        out_specs=pl.BlockSpec((SUBLANES, ts, LANES), lambda i, j, t: (i, t, j)),
        out_shape=jax.ShapeDtypeStruct((b, s, w), F32),
        scratch_shapes=[
            pltpu.VMEM((rows, LANES), F32),
            pltpu.VMEM((rows, 2 * SLAB_STATE), F32),
            pltpu.VMEM((SUBLANES, 2 * SLAB_STATE), F32),
            pltpu.VMEM((rows, LANES), F32),
        ],
        compiler_params=_params("parallel", "parallel", "arbitrary"),
        name="s5_scan",
    )(u, bmat, a_re, a_im, cmat, dskip)


def _s5_params(a_re, a_im, log_dt, b_re, b_im, c_re, c_im, d_skip):
    a = lax.complex(a_re.astype(F32), a_im.astype(F32))
    dt = jnp.exp(log_dt.astype(F32))[:, None]
    a_bar = jnp.exp(a * dt)
    b_bar = ((a_bar - 1.0) / a)[..., None] * lax.complex(b_re.astype(F32), b_im.astype(F32))
    eye = jnp.eye(GROUPS_PER_SLAB, dtype=F32)

    def block_diag(m):
        g, r, c = m.shape
        m = m.reshape(N_SLABS, GROUPS_PER_SLAB, r, c)
        return jnp.einsum('sgrc,gh->sgrhc', m, eye).reshape(N_SLABS, GROUPS_PER_SLAB * r, GROUPS_PER_SLAB * c)

    b_t = jnp.swapaxes(b_bar, 1, 2)
    bmat = jnp.concatenate([block_diag(jnp.real(b_t)), block_diag(jnp.imag(b_t))], axis=-1)
    c_re_t = jnp.swapaxes(c_re.astype(F32), 1, 2)
    c_im_t = jnp.swapaxes(c_im.astype(F32), 1, 2)
    cmat = jnp.concatenate([block_diag(c_re_t), block_diag(-c_im_t)], axis=1)
    return (bmat.astype(BF16),
            jnp.real(a_bar).reshape(N_SLABS, 1, SLAB_STATE),
            jnp.imag(a_bar).reshape(N_SLABS, 1, SLAB_STATE),
            cmat.astype(BF16),
            d_skip.astype(F32).reshape(N_SLABS, 1, LANES))


def _ffn_block(x, g_pre, w_gate, w_up, w_down, g_post):
    hn = _rms_norm(x, g_pre).astype(BF16)
    gate = jnp.dot(hn, w_gate, preferred_element_type=F32)
    up = jnp.dot(hn, w_up, preferred_element_type=F32)
    act = (jax.nn.silu(gate) * up).astype(BF16)
    ff = jnp.dot(act, w_down, preferred_element_type=F32)
    return x + _rms_norm(ff, g_post)


def _ffn_kernel(x_ref, gpre_ref, wg_ref, wu_ref, wd_ref, gpost_ref, o_ref):
    o_ref[...] = _ffn_block(x_ref[...], gpre_ref[...], wg_ref[...], wu_ref[...], wd_ref[...], gpost_ref[...])


def _ffn(x, g_pre, w_gate, w_up, w_down, g_post, layer, tm):
    t, d = x.shape
    row_spec = pl.BlockSpec((tm, d), lambda i: (i, 0))
    return pl.pallas_call(
        _ffn_kernel,
        grid=(t // tm,),
        in_specs=[row_spec, _const_spec((1, d)), _layer_spec(w_gate, layer), _layer_spec(w_up, layer),
                  _layer_spec(w_down, layer), _const_spec((1, d))],
        out_specs=row_spec,
        out_shape=jax.ShapeDtypeStruct((t, d), F32),
        compiler_params=_params("parallel"),
        name="ffn",
    )(x, g_pre, w_gate, w_up, w_down, g_post)


def _even_post_kernel(att_ref, yg_ref, x_ref, wglu_ref, wout_ref, gpost_ref,
                      gfpre_ref, wg_ref, wu_ref, wd_ref, gfpost_ref, o_ref):
    yg = yg_ref[...]
    glu = jnp.dot(yg.astype(BF16), wglu_ref[...], preferred_element_type=F32)
    ssm = (yg * jax.nn.sigmoid(glu)).astype(BF16)
    att = jnp.concatenate([att_ref[0, j] for j in range(N_SLABS)], axis=1).astype(BF16)
    mix = (jnp.dot(att, wout_ref[:ATT_WIDTH, :], preferred_element_type=F32)
           + jnp.dot(ssm, wout_ref[ATT_WIDTH:, :], preferred_element_type=F32))
    x1 = x_ref[...] + _rms_norm(mix, gpost_ref[...])
    o_ref[...] = _ffn_block(x1, gfpre_ref[...], wg_ref[...], wu_ref[...], wd_ref[...], gfpost_ref[...])


def _even_post_ffn(att, yg, x, w_glu, w_out, g_post, g_fpre, w_gate, w_up, w_down, g_fpost, layer, tm):
    t, d = x.shape
    tiles_per_seq = att.shape[2] // tm
    row = lambda w: pl.BlockSpec((tm, w), lambda i: (i, 0))
    att_spec = pl.BlockSpec((1, N_SLABS, tm, LANES), lambda i: (i // tiles_per_seq, 0, i % tiles_per_seq, 0))
    return pl.pallas_call(
        _even_post_kernel,
        grid=(t // tm,),
        in_specs=[att_spec, row(SSM_WIDTH), row(d), _const_spec(w_glu.shape), _const_spec(w_out.shape),
                  _const_spec((1, d)), _const_spec((1, d)), _layer_spec(w_gate, layer),
                  _layer_spec(w_up, layer), _layer_spec(w_down, layer), _const_spec((1, d))],
        out_specs=row(d),
        out_shape=jax.ShapeDtypeStruct((t, d), F32),
        compiler_params=_params("parallel"),
        name="even_post_ffn",
    )(att, yg, x, w_glu, w_out, g_post, g_fpre, w_gate, w_up, w_down, g_fpost)


def _rglru_kernel(x_ref, gpre_ref, win_ref, convw_ref, convb_ref, wr_ref, br_ref, wi_ref, bi_ref, lam_ref,
                  wout_ref, gpost_ref, o_ref, hn_tb, xprev, hcar, a_s, b_s, mix_tb, *, ts):
    rows = SUBLANES * ts
    n_slabs = LRU_WIDTH // LANES
    tail = (CONV_WIDTH - 1) * SUBLANES

    @pl.when(pl.program_id(1) == 0)
    def _():
        xprev[...] = jnp.zeros_like(xprev)
        hcar[...] = jnp.zeros_like(hcar)

    for b in range(SUBLANES):
        hn = _rms_norm(x_ref[b], gpre_ref[...])
        for j in range(n_slabs):
            hn_tb[j, pl.ds(b, ts, stride=SUBLANES), :] = hn[:, j * LANES:(j + 1) * LANES]
    hn_rows = jnp.concatenate([hn_tb[j] for j in range(n_slabs)], axis=1).astype(BF16)
    z = jnp.dot(hn_rows, win_ref[...], preferred_element_type=F32)
    xb = z[:, :LRU_WIDTH]
    gate_half = z[:, LRU_WIDTH:]

    xext = jnp.concatenate([xprev[...], xb], axis=0)
    xc_half = convb_ref[...]
    for kk in range(CONV_WIDTH):
        xc_half = xc_half + convw_ref[kk:kk + 1, :] * xext[kk * SUBLANES:kk * SUBLANES + rows]
    xprev[...] = xb[rows - tail:]

    xcb = xc_half.astype(BF16)

    def block_tanh(w_ref, bias_ref):
        parts = [jnp.dot(xcb[:, h * LRU_BLOCK_DIM:(h + 1) * LRU_BLOCK_DIM], w_ref[h],
                         preferred_element_type=F32) for h in range(LRU_BLOCKS)]
        return jnp.tanh(jnp.concatenate(parts, axis=1) + bias_ref[...])

    t_r = block_tanh(wr_ref, br_ref)
    t_i = block_tanh(wi_ref, bi_ref)
    c_half = (-0.5 * RG_C) * jax.nn.softplus(-lam_ref[...])
    log_a = t_r * c_half + c_half
    a = jnp.exp(log_a)
    v = -jnp.tanh(log_a) * (1.0 + a * a)
    mult = jnp.where(v > 0.0, v * lax.rsqrt(v), 0.0)
    a_s[...] = a
    b_s[...] = (t_i + 1.0) * (xc_half * mult)

    def step(t, h):
        row = pl.multiple_of(t * SUBLANES, SUBLANES)
        h = a_s[pl.ds(row, SUBLANES), :] * h + b_s[pl.ds(row, SUBLANES), :]
        b_s[pl.ds(row, SUBLANES), :] = h
        return h

    hcar[...] = lax.fori_loop(0, ts, step, hcar[...], unroll=True)

    inner = gate_half * (GELU_K1 + GELU_K3 * (gate_half * gate_half))
    y = ((b_s[...] * gate_half) * (1.0 + jnp.tanh(inner))).astype(BF16)
    mix = _rms_norm(jnp.dot(y, wout_ref[...], preferred_element_type=F32), gpost_ref[...])
    for j in range(n_slabs):
        mix_tb[j] = mix[:, j * LANES:(j + 1) * LANES]
    for b in range(SUBLANES):
        mix_b = jnp.concatenate([mix_tb[j, pl.ds(b, ts, stride=SUBLANES), :] for j in range(n_slabs)], axis=1)
        o_ref[b] = x_ref[b] + mix_b


def _rglru(x, g_pre, w_in, conv_w, conv_b, w_r, b_r, w_i, b_i, lam, w_out, g_post, ts):
    b, s, d = x.shape
    rows = SUBLANES * ts
    blk = pl.BlockSpec((SUBLANES, ts, d), lambda i, t: (i, t, 0))
    col_scale = jnp.where(jnp.arange(2 * LRU_WIDTH) < LRU_WIDTH, 1.0, 0.5).astype(F32)
    w_in = (w_in * col_scale).astype(BF16)
    conv_w, conv_b, b_r, b_i = 0.5 * conv_w, 0.5 * conv_b, 0.5 * b_r, 0.5 * b_i
    return pl.pallas_call(
        functools.partial(_rglru_kernel, ts=ts),
        grid=(b // SUBLANES, s // ts),
        in_specs=[blk, _const_spec((1, d)), _const_spec(w_in.shape), _const_spec(conv_w.shape),
                  _const_spec((1, d)), _const_spec(w_r.shape), _const_spec((1, d)), _const_spec(w_i.shape),
                  _const_spec((1, d)), _const_spec((1, d)), _const_spec(w_out.shape), _const_spec((1, d))],
        out_specs=blk,
        out_shape=jax.ShapeDtypeStruct((b, s, d), F32),
        scratch_shapes=[
            pltpu.VMEM((d // LANES, rows, LANES), F32),
            pltpu.VMEM(((CONV_WIDTH - 1) * SUBLANES, LRU_WIDTH), F32),
            pltpu.VMEM((SUBLANES, LRU_WIDTH), F32),
            pltpu.VMEM((rows, LRU_WIDTH), F32),
            pltpu.VMEM((rows, LRU_WIDTH), F32),
            pltpu.VMEM((d // LANES, rows, LANES), F32),
        ],
        compiler_params=_params("parallel", "arbitrary"),
        name="rglru_block",
    )(x, g_pre, w_in, conv_w, conv_b, w_r, b_r, w_i, b_i, lam, w_out, g_post)


def _tiles(s):
    return dict(even_in_rows=min(1024, s), s5_steps=min(512, s), ffn_rows=512, rglru_steps=min(128, s))


def kernel(x, positions, norm_mix_pre, norm_mix_post, norm_ffn_pre, norm_ffn_post, ev_w_in, ev_w_out,
           s5_a_re, s5_a_im, s5_log_dt, s5_b_re, s5_b_im, s5_c_re, s5_c_im, s5_d, s5_w_glu,
           od_w_in, od_w_out, rg_conv_w, rg_conv_b, rg_w_r, rg_b_r, rg_w_i, rg_b_i, rg_lam,
           ffn_w_gate, ffn_w_up, ffn_w_down):
    b, s, d = x.shape
    assert d == D_MODEL and b % SUBLANES == 0 and s % (16 * ATT_BLOCK) == 0
    tiles = _tiles(s)
    row = lambda v: v.astype(F32).reshape(1, -1)
    bf = lambda w: w.astype(BF16)
    w_gate, w_up, w_down = bf(ffn_w_gate), bf(ffn_w_up), bf(ffn_w_down)

    q, k, v, u = _even_in(x, positions, row(norm_mix_pre[0]), bf(ev_w_in[0]), tiles["even_in_rows"])
    att = _attention(q, k, v)
    yg = _s5(u, *_s5_params(s5_a_re[0], s5_a_im[0], s5_log_dt[0], s5_b_re[0], s5_b_im[0],
                            s5_c_re[0], s5_c_im[0], s5_d[0]), tiles["s5_steps"])
    x2 = _even_post_ffn(att, yg.reshape(b * s, SSM_WIDTH), x.reshape(b * s, d),
                        bf(s5_w_glu[0]), bf(ev_w_out[0]), row(norm_mix_post[0]),
                        row(norm_ffn_pre[0]), w_gate, w_up, w_down, row(norm_ffn_post[0]), 0, tiles["ffn_rows"])

    x3 = _rglru(x2.reshape(b, s, d), row(norm_mix_pre[1]), od_w_in[0], rg_conv_w[0].astype(F32),
                row(rg_conv_b[0]), bf(rg_w_r[0]), row(rg_b_r[0]), bf(rg_w_i[0]), row(rg_b_i[0]),
                row(rg_lam[0]), bf(od_w_out[0]), row(norm_mix_post[1]), tiles["rglru_steps"])
    x4 = _ffn(x3.reshape(b * s, d), row(norm_ffn_pre[1]), w_gate, w_up, w_down, row(norm_ffn_post[1]), 1,
              tiles["ffn_rows"])
    return x4.reshape(b, s, d)
```

```python
import functools
import math

import jax
import jax.numpy as jnp
from jax import lax
from jax.experimental import pallas as pl
from jax.experimental.pallas import tpu as pltpu

F32 = jnp.float32
BF16 = jnp.bfloat16

D_MODEL = 1024
HEAD_DIM = 64
ATT_WIDTH = 512
ROPE_DIM = 16
ROPE_THETA = 500000.0
ATT_BLOCK = 128
NEG_INF = -1e30
SSM_WIDTH = 512
SSM_GROUP = 16
SSM_GROUPS = 32
SSM_STATE = 64
LRU_WIDTH = 1024
LRU_BLOCKS = 4
LRU_BLOCK_DIM = 256
CONV_WIDTH = 4
RG_C = 8.0
FFN_HIDDEN = 2816
NORM_EPS = 1e-6

LANES = 128
SUBLANES = 8
N_SLABS = ATT_WIDTH // LANES
GROUPS_PER_SLAB = LANES // SSM_GROUP
SLAB_STATE = GROUPS_PER_SLAB * SSM_STATE
GELU_K1 = 2.0 * math.sqrt(2.0 / math.pi)
GELU_K3 = 8.0 * math.sqrt(2.0 / math.pi) * 0.044715
ATT_SLABS = 2
DIL_MID = 4
ROT_PACK = 4
ROT_PACK_LANES = LANES // ROT_PACK
Q_SCALE = HEAD_DIM ** -0.5 * math.log2(math.e)
VMEM_LIMIT_BYTES = 56 * 1024 * 1024


def _rms_norm(x, g):
    return x * lax.rsqrt(jnp.mean(x * x, axis=-1, keepdims=True) + NORM_EPS) * g


def _const_spec(shape):
    n = len(shape)
    return pl.BlockSpec(shape, lambda *_: (0,) * n, pipeline_mode=pl.Buffered(1))


def _layer_spec(stacked, layer):
    return pl.BlockSpec((None,) + stacked.shape[1:], lambda *_: (layer, 0, 0), pipeline_mode=pl.Buffered(1))


def _params(*sem):
    return pltpu.CompilerParams(dimension_semantics=sem, vmem_limit_bytes=VMEM_LIMIT_BYTES)


def _even_in_kernel(x_ref, pos_ref, g_ref, w_ref, invf_ref, sgn_ref, q_ref, k_ref, v_ref, u_ref, cos_s, sin_s):
    packed_rows = pos_ref.shape[1]
    ang = pos_ref[0].astype(F32) * invf_ref[...]
    cos4 = jnp.cos(ang)
    sin4 = jnp.sin(ang) * sgn_ref[...]
    lane128 = lax.broadcasted_iota(jnp.int32, (1, LANES), 1)
    rot_lo = lane128 < ROPE_DIM
    rot_hi = (lane128 >= HEAD_DIM) & (lane128 < HEAD_DIM + ROPE_DIM)
    for r in range(ROT_PACK):
        for packed, dst, ident in ((cos4, cos_s, 1.0), (sin4, sin_s, 0.0)):
            t = packed if r == 0 else pltpu.roll(packed, LANES - ROT_PACK_LANES * r, 1)
            t = jnp.where(rot_lo, t, ident)
            t = jnp.where(rot_hi, pltpu.roll(t, HEAD_DIM, 1), t)
            dst[pl.ds(r, packed_rows, stride=ROT_PACK), :] = t
    cos = cos_s[...]
    sin = sin_s[...]
    hn = _rms_norm(x_ref[0], g_ref[...])
    z = jnp.dot(hn.astype(BF16), w_ref[...], preferred_element_type=F32)
    lane = lax.broadcasted_iota(jnp.int32, (1, LANES), 1) % HEAD_DIM
    first_half = lane < (ROPE_DIM // 2)

    def rope(t):
        partner = jnp.where(first_half, pltpu.roll(t, LANES - ROPE_DIM // 2, 1),
                            pltpu.roll(t, ROPE_DIM // 2, 1))
        return t * cos + partner * sin

    for j in range(N_SLABS):
        lo = j * LANES
        q_ref[0, j] = rope(z[:, lo:lo + LANES]) * Q_SCALE
        k_ref[0, j] = rope(z[:, ATT_WIDTH + lo:ATT_WIDTH + lo + LANES])
        v_ref[0, j] = z[:, 2 * ATT_WIDTH + lo:2 * ATT_WIDTH + lo + LANES]
    u_ref[0] = z[:, 3 * ATT_WIDTH:]


def _even_in(x, positions, g, w_in, tm):
    b, s, d = x.shape
    half = ROPE_DIM // 2
    inv_freq = ROPE_THETA ** (-(jnp.arange(half, dtype=F32) * 2.0 / ROPE_DIM))
    assert ROPE_DIM <= ROT_PACK_LANES and 2 * HEAD_DIM == LANES and tm % (ROT_PACK * SUBLANES) == 0
    lane = jnp.arange(LANES) % ROT_PACK_LANES
    invf = jnp.where(lane < ROPE_DIM, inv_freq[lane % half], 0.0).astype(F32)[None, :]
    sgn = jnp.where(lane < half, -1.0, jnp.where(lane < ROPE_DIM, 1.0, 0.0)).astype(F32)[None, :]
    pos4 = jnp.repeat(positions.reshape(b, s // ROT_PACK, ROT_PACK), ROT_PACK_LANES, axis=-1)
    slab = jax.ShapeDtypeStruct((b, N_SLABS, s, LANES), F32)
    slab_spec = pl.BlockSpec((1, N_SLABS, tm, LANES), lambda i, j: (i, 0, j, 0))
    return pl.pallas_call(
        _even_in_kernel,
        grid=(b, s // tm),
        in_specs=[
            pl.BlockSpec((1, tm, d), lambda i, j: (i, j, 0)),
            pl.BlockSpec((1, tm // ROT_PACK, LANES), lambda i, j: (i, j, 0)),
            _const_spec((1, d)),
            _const_spec(w_in.shape),
            _const_spec((1, LANES)),
            _const_spec((1, LANES)),
        ],
        out_specs=[slab_spec, slab_spec, slab_spec,
                   pl.BlockSpec((1, tm, SSM_WIDTH), lambda i, j: (i, j, 0))],
        out_shape=[slab, slab, slab, jax.ShapeDtypeStruct((b, s, SSM_WIDTH), F32)],
        scratch_shapes=[pltpu.VMEM((tm, LANES), F32)] * 2,
        compiler_params=_params("parallel", "parallel"),
        name="even_in",
    )(x, pos4, g, w_in, invf, sgn)


def _attn_kernel(q_ref, k_ref, v_ref, o_ref, q4, k4, v4, on_ref, lse_ref, *, seq):
    blk = ATT_BLOCK
    sub = seq // DIL_MID
    head0 = lax.broadcasted_iota(jnp.int32, (1, LANES), 1) < HEAD_DIM

    def bias(nk):
        qi = lax.broadcasted_iota(jnp.int32, (blk, nk), 0) + (nk - blk)
        ki = lax.broadcasted_iota(jnp.int32, (blk, nk), 1)
        dist = qi - ki
        return jnp.where((dist >= 0) & (dist <= blk), 0.0, NEG_INF).astype(F32)

    biases = {blk: bias(blk), 2 * blk: bias(2 * blk)}

    def unit(qb, kb, vb):
        nk = kb.shape[0]
        k16 = kb.astype(BF16)
        v_ext = jnp.concatenate([vb.astype(BF16), jnp.ones((nk, LANES), BF16)], axis=1)
        outs = []
        for qh in (jnp.where(head0, qb, 0.0), jnp.where(head0, 0.0, qb)):
            s = lax.dot_general(qh.astype(BF16), k16, (((1,), (1,)), ((), ())), preferred_element_type=F32)
            s = s + biases[nk]
            half_max = s if nk == blk else jnp.maximum(s[:, :blk], s[:, blk:])
            m = jnp.max(half_max, axis=-1, keepdims=True)
            p = jnp.exp2(s - m).astype(BF16)
            outs.append((jnp.dot(p, v_ext, preferred_element_type=F32), m))
        (o0, m0), (o1, m1) = outs
        acc = jnp.where(head0, o0[:, :LANES], o1[:, :LANES])
        l = jnp.where(head0, o0[:, LANES:], o1[:, LANES:])
        return acc / l, jnp.where(head0, m0, m1) + jnp.log2(l)

    def head_pair(sl):
        def run_unit(br, src, q_rows, kv_rows, out_rows):
            qs, ks, vs = src
            on, lse = unit(qs(q_rows), ks(kv_rows), vs(kv_rows))
            on_ref[sl, br, out_rows, :] = on
            lse_ref[sl, br, out_rows, :] = lse

        natural = tuple((lambda rows, r=r: r[0, sl, rows, :]) for r in (q_ref, k_ref, v_ref))
        residue_major = tuple((lambda rows, r=r: r[sl, rows, :]) for r in (q4, k4, v4))

        for r in range(DIL_MID):
            for src, dst in ((q_ref, q4), (k_ref, k4), (v_ref, v4)):
                dst[sl, pl.ds(r * sub, sub), :] = src[0, sl, pl.ds(r, sub, stride=DIL_MID), :]

        for n in range(seq // blk):
            q_rows = pl.ds(n * blk, blk)
            run_unit(0, natural, q_rows, q_rows if n == 0 else pl.ds((n - 1) * blk, 2 * blk), q_rows)
        for r in range(DIL_MID):
            for n in range(sub // blk):
                q_rows = pl.ds(r * sub + n * blk, blk)
                run_unit(1, residue_major, q_rows,
                         q_rows if n == 0 else pl.ds(r * sub + (n - 1) * blk, 2 * blk), q_rows)
        assert sub // DIL_MID == blk
        for far in range(DIL_MID * DIL_MID):
            rows = pl.ds((far % DIL_MID) * sub + far // DIL_MID, blk, stride=DIL_MID)
            run_unit(2, residue_major, rows, rows, rows)

        for r in range(DIL_MID):
            nat = pl.ds(r, sub, stride=DIL_MID)
            rm = pl.ds(r * sub, sub)
            l0, l1, l2 = lse_ref[sl, 0, nat, :], lse_ref[sl, 1, rm, :], lse_ref[sl, 2, rm, :]
            mx = jnp.maximum(jnp.maximum(l0, l1), l2)
            w0, w1, w2 = jnp.exp2(l0 - mx), jnp.exp2(l1 - mx), jnp.exp2(l2 - mx)
            num = w0 * on_ref[sl, 0, nat, :] + w1 * on_ref[sl, 1, rm, :] + w2 * on_ref[sl, 2, rm, :]
            o_ref[0, sl, nat, :] = num / (w0 + w1 + w2)

    for sl in range(ATT_SLABS):
        head_pair(sl)


def _attention(q, k, v):
    b, n_slabs, s, _ = q.shape
    spec = pl.BlockSpec((1, ATT_SLABS, s, LANES), lambda i, j: (i, j, 0, 0))
    return pl.pallas_call(
        functools.partial(_attn_kernel, seq=s),
        grid=(b, n_slabs // ATT_SLABS),
        in_specs=[spec, spec, spec],
        out_specs=spec,
        out_shape=jax.ShapeDtypeStruct((b, n_slabs, s, LANES), F32),
        scratch_shapes=([pltpu.VMEM((ATT_SLABS, s, LANES), F32)] * 3
                        + [pltpu.VMEM((ATT_SLABS, 3, s, LANES), F32)] * 2),
        compiler_params=_params("parallel", "parallel"),
        name="dilated_attention",
    )(q, k, v)


def _s5_kernel(u_ref, bm_ref, are_ref, aim_ref, cm_ref, d_ref, o_ref, utb, hs, hcar, ys, *, ts):
    @pl.when(pl.program_id(2) == 0)
    def _():
        hcar[...] = jnp.zeros_like(hcar)

    for b in range(SUBLANES):
        utb[pl.ds(b, ts, stride=SUBLANES), :] = u_ref[b]
    u_tb = utb[...]
    hs[...] = jnp.dot(u_tb.astype(BF16), bm_ref[0], preferred_element_type=F32)
    ar = jnp.broadcast_to(are_ref[0], (SUBLANES, SLAB_STATE))
    ai = jnp.broadcast_to(aim_ref[0], (SUBLANES, SLAB_STATE))

    def step(t, carry):
        hr, hi = carry
        row = pl.multiple_of(t * SUBLANES, SUBLANES)
        xr = hs[pl.ds(row, SUBLANES), :SLAB_STATE]
        xi = hs[pl.ds(row, SUBLANES), SLAB_STATE:]
        nr = ar * hr - ai * hi + xr
        ni = ar * hi + ai * hr + xi
        hs[pl.ds(row, SUBLANES), :SLAB_STATE] = nr
        hs[pl.ds(row, SUBLANES), SLAB_STATE:] = ni
        return nr, ni

    hr, hi = lax.fori_loop(0, ts, step, (hcar[:, :SLAB_STATE], hcar[:, SLAB_STATE:]), unroll=True)
    hcar[:, :SLAB_STATE] = hr
    hcar[:, SLAB_STATE:] = hi

    y = jnp.dot(hs[...].astype(BF16), cm_ref[0], preferred_element_type=F32)
    ys[...] = jax.nn.gelu(y + d_ref[0] * u_tb)
    for b in range(SUBLANES):
        o_ref[b] = ys[pl.ds(b, ts, stride=SUBLANES), :]


def _s5(u, bmat, a_re, a_im, cmat, dskip, ts):
    b, s, w = u.shape
    rows = SUBLANES * ts
    return pl.pallas_call(
        functools.partial(_s5_kernel, ts=ts),
        grid=(b // SUBLANES, N_SLABS, s // ts),
        in_specs=[
            pl.BlockSpec((SUBLANES, ts, LANES), lambda i, j, t: (i, t, j)),
            pl.BlockSpec((1, LANES, 2 * SLAB_STATE), lambda i, j, t: (j, 0, 0)),
            pl.BlockSpec((1, 1, SLAB_STATE), lambda i, j, t: (j, 0, 0)),
            pl.BlockSpec((1, 1, SLAB_STATE), lambda i, j, t: (j, 0, 0)),
            pl.BlockSpec((1, 2 * SLAB_STATE, LANES), lambda i, j, t: (j, 0, 0)),
            pl.BlockSpec((1, 1, LANES), lambda i, j, t: (j, 0, 0)),
        ],
        out_specs=pl.BlockSpec((SUBLANES, ts, LANES), lambda i, j, t: (i, t, j)),
        out_shape=jax.ShapeDtypeStruct((b, s, w), F32),
        scratch_shapes=[
            pltpu.VMEM((rows, LANES), F32),
            pltpu.VMEM((rows, 2 * SLAB_STATE), F32),
            pltpu.VMEM((SUBLANES, 2 * SLAB_STATE), F32),
            pltpu.VMEM((rows, LANES), F32),
        ],
        compiler_params=_params("parallel", "parallel", "arbitrary"),
        name="s5_scan",
    )(u, bmat, a_re, a_im, cmat, dskip)


def _s5_params(a_re, a_im, log_dt, b_re, b_im, c_re, c_im, d_skip):
    a = lax.complex(a_re.astype(F32), a_im.astype(F32))
    dt = jnp.exp(log_dt.astype(F32))[:, None]
    a_bar = jnp.exp(a * dt)
    b_bar = ((a_bar - 1.0) / a)[..., None] * lax.complex(b_re.astype(F32), b_im.astype(F32))
    eye = jnp.eye(GROUPS_PER_SLAB, dtype=F32)

    def block_diag(m):
        g, r, c = m.shape
        m = m.reshape(N_SLABS, GROUPS_PER_SLAB, r, c)
        return jnp.einsum('sgrc,gh->sgrhc', m, eye).reshape(N_SLABS, GROUPS_PER_SLAB * r, GROUPS_PER_SLAB * c)

    b_t = jnp.swapaxes(b_bar, 1, 2)
    bmat = jnp.concatenate([block_diag(jnp.real(b_t)), block_diag(jnp.imag(b_t))], axis=-1)
    c_re_t = jnp.swapaxes(c_re.astype(F32), 1, 2)
    c_im_t = jnp.swapaxes(c_im.astype(F32), 1, 2)
    cmat = jnp.concatenate([block_diag(c_re_t), block_diag(-c_im_t)], axis=1)
    return (bmat.astype(BF16),
            jnp.real(a_bar).reshape(N_SLABS, 1, SLAB_STATE),
            jnp.imag(a_bar).reshape(N_SLABS, 1, SLAB_STATE),
            cmat.astype(BF16),
            d_skip.astype(F32).reshape(N_SLABS, 1, LANES))


def _ffn_block(x, g_pre, w_gate, w_up, w_down, g_post):
    hn = _rms_norm(x, g_pre).astype(BF16)
    gate = jnp.dot(hn, w_gate, preferred_element_type=F32)
    up = jnp.dot(hn, w_up, preferred_element_type=F32)
    act = (jax.nn.silu(gate) * up).astype(BF16)
    ff = jnp.dot(act, w_down, preferred_element_type=F32)
    return x + _rms_norm(ff, g_post)


def _ffn_kernel(x_ref, gpre_ref, wg_ref, wu_ref, wd_ref, gpost_ref, o_ref):
    o_ref[...] = _ffn_block(x_ref[...], gpre_ref[...], wg_ref[...], wu_ref[...], wd_ref[...], gpost_ref[...])


def _ffn(x, g_pre, w_gate, w_up, w_down, g_post, layer, tm):
    t, d = x.shape
    row_spec = pl.BlockSpec((tm, d), lambda i: (i, 0))
    return pl.pallas_call(
        _ffn_kernel,
        grid=(t // tm,),
        in_specs=[row_spec, _const_spec((1, d)), _layer_spec(w_gate, layer), _layer_spec(w_up, layer),
                  _layer_spec(w_down, layer), _const_spec((1, d))],
        out_specs=row_spec,
        out_shape=jax.ShapeDtypeStruct((t, d), F32),
        compiler_params=_params("parallel"),
        name="ffn",
    )(x, g_pre, w_gate, w_up, w_down, g_post)


def _even_post_kernel(att_ref, yg_ref, x_ref, wglu_ref, wout_ref, gpost_ref, o_ref):
    yg = yg_ref[...]
    glu = jnp.dot(yg.astype(BF16), wglu_ref[...], preferred_element_type=F32)
    ssm = (yg * jax.nn.sigmoid(glu)).astype(BF16)
    att = jnp.concatenate([att_ref[0, j] for j in range(N_SLABS)], axis=1).astype(BF16)
    mix = (jnp.dot(att, wout_ref[:ATT_WIDTH, :], preferred_element_type=F32)
           + jnp.dot(ssm, wout_ref[ATT_WIDTH:, :], preferred_element_type=F32))
    o_ref[...] = x_ref[...] + _rms_norm(mix, gpost_ref[...])


def _even_post(att, yg, x, w_glu, w_out, g_post, tm):
    t, d = x.shape
    tiles_per_seq = att.shape[2] // tm
    row = lambda w: pl.BlockSpec((tm, w), lambda i: (i, 0))
    att_spec = pl.BlockSpec((1, N_SLABS, tm, LANES), lambda i: (i // tiles_per_seq, 0, i % tiles_per_seq, 0))
    return pl.pallas_call(
        _even_post_kernel,
        grid=(t // tm,),
        in_specs=[att_spec, row(SSM_WIDTH), row(d), _const_spec(w_glu.shape), _const_spec(w_out.shape),
                  _const_spec((1, d))],
        out_specs=row(d),
        out_shape=jax.ShapeDtypeStruct((t, d), F32),
        compiler_params=_params("parallel"),
        name="even_post",
    )(att, yg, x, w_glu, w_out, g_post)


def _rglru_kernel(x_ref, gpre_ref, win_ref, convw_ref, convb_ref, wr_ref, br_ref, wi_ref, bi_ref, lam_ref,
                  wout_ref, gpost_ref, o_ref, hn_tb, xprev, hcar, a_s, b_s, mix_tb, *, ts):
    rows = SUBLANES * ts
    n_slabs = LRU_WIDTH // LANES
    tail = (CONV_WIDTH - 1) * SUBLANES

    @pl.when(pl.program_id(1) == 0)
    def _():
        xprev[...] = jnp.zeros_like(xprev)
        hcar[...] = jnp.zeros_like(hcar)

    for b in range(SUBLANES):
        hn = _rms_norm(x_ref[b], gpre_ref[...])
        for j in range(n_slabs):
            hn_tb[j, pl.ds(b, ts, stride=SUBLANES), :] = hn[:, j * LANES:(j + 1) * LANES]
    hn_rows = jnp.concatenate([hn_tb[j] for j in range(n_slabs)], axis=1).astype(BF16)
    z = jnp.dot(hn_rows, win_ref[...], preferred_element_type=F32)
    xb = z[:, :LRU_WIDTH]
    gate_half = z[:, LRU_WIDTH:]

    xext = jnp.concatenate([xprev[...], xb], axis=0)
    xc_half = convb_ref[...]
    for kk in range(CONV_WIDTH):
        xc_half = xc_half + convw_ref[kk:kk + 1, :] * xext[kk * SUBLANES:kk * SUBLANES + rows]
    xprev[...] = xb[rows - tail:]

    xcb = xc_half.astype(BF16)

    def block_tanh(w_ref, bias_ref):
        parts = [jnp.dot(xcb[:, h * LRU_BLOCK_DIM:(h + 1) * LRU_BLOCK_DIM], w_ref[h],
                         preferred_element_type=F32) for h in range(LRU_BLOCKS)]
        return jnp.tanh(jnp.concatenate(parts, axis=1) + bias_ref[...])

    t_r = block_tanh(wr_ref, br_ref)
    t_i = block_tanh(wi_ref, bi_ref)
    c_half = (-0.5 * RG_C) * jax.nn.softplus(-lam_ref[...])
    log_a = t_r * c_half + c_half
    a = jnp.exp(log_a)
    v = -jnp.tanh(log_a) * (1.0 + a * a)
    mult = jnp.where(v > 0.0, v * lax.rsqrt(v), 0.0)
    a_s[...] = a
    b_s[...] = (t_i + 1.0) * (xc_half * mult)

    def step(t, h):
        row = pl.multiple_of(t * SUBLANES, SUBLANES)
        h = a_s[pl.ds(row, SUBLANES), :] * h + b_s[pl.ds(row, SUBLANES), :]
        b_s[pl.ds(row, SUBLANES), :] = h
        return h

    hcar[...] = lax.fori_loop(0, ts, step, hcar[...], unroll=True)

    inner = gate_half * (GELU_K1 + GELU_K3 * (gate_half * gate_half))
    y = ((b_s[...] * gate_half) * (1.0 + jnp.tanh(inner))).astype(BF16)
    mix = _rms_norm(jnp.dot(y, wout_ref[...], preferred_element_type=F32), gpost_ref[...])
    for j in range(n_slabs):
        mix_tb[j] = mix[:, j * LANES:(j + 1) * LANES]
    for b in range(SUBLANES):
        mix_b = jnp.concatenate([mix_tb[j, pl.ds(b, ts, stride=SUBLANES), :] for j in range(n_slabs)], axis=1)
        o_ref[b] = x_ref[b] + mix_b


def _rglru(x, g_pre, w_in, conv_w, conv_b, w_r, b_r, w_i, b_i, lam, w_out, g_post, ts):
    b, s, d = x.shape
    rows = SUBLANES * ts
    blk = pl.BlockSpec((SUBLANES, ts, d), lambda i, t: (i, t, 0))
    col_scale = jnp.where(jnp.arange(2 * LRU_WIDTH) < LRU_WIDTH, 1.0, 0.5).astype(F32)
    w_in = (w_in * col_scale).astype(BF16)
    conv_w, conv_b, b_r, b_i = 0.5 * conv_w, 0.5 * conv_b, 0.5 * b_r, 0.5 * b_i
    return pl.pallas_call(
        functools.partial(_rglru_kernel, ts=ts),
        grid=(b // SUBLANES, s // ts),
        in_specs=[blk, _const_spec((1, d)), _const_spec(w_in.shape), _const_spec(conv_w.shape),
                  _const_spec((1, d)), _const_spec(w_r.shape), _const_spec((1, d)), _const_spec(w_i.shape),
                  _const_spec((1, d)), _const_spec((1, d)), _const_spec(w_out.shape), _const_spec((1, d))],
        out_specs=blk,
        out_shape=jax.ShapeDtypeStruct((b, s, d), F32),
        scratch_shapes=[
            pltpu.VMEM((d // LANES, rows, LANES), F32),
            pltpu.VMEM(((CONV_WIDTH - 1) * SUBLANES, LRU_WIDTH), F32),
            pltpu.VMEM((SUBLANES, LRU_WIDTH), F32),
            pltpu.VMEM((rows, LRU_WIDTH), F32),
            pltpu.VMEM((rows, LRU_WIDTH), F32),
            pltpu.VMEM((d // LANES, rows, LANES), F32),
        ],
        compiler_params=_params("parallel", "arbitrary"),
        name="rglru_block",
    )(x, g_pre, w_in, conv_w, conv_b, w_r, b_r, w_i, b_i, lam, w_out, g_post)


def _tiles(s):
    return dict(even_in_rows=min(1024, s), s5_steps=min(512, s), post_rows=min(1024, s), ffn_rows=512, rglru_steps=min(128, s))


def kernel(x, positions, norm_mix_pre, norm_mix_post, norm_ffn_pre, norm_ffn_post, ev_w_in, ev_w_out,
           s5_a_re, s5_a_im, s5_log_dt, s5_b_re, s5_b_im, s5_c_re, s5_c_im, s5_d, s5_w_glu,
           od_w_in, od_w_out, rg_conv_w, rg_conv_b, rg_w_r, rg_b_r, rg_w_i, rg_b_i, rg_lam,
           ffn_w_gate, ffn_w_up, ffn_w_down):
    b, s, d = x.shape
    assert d == D_MODEL and b % SUBLANES == 0 and s % (16 * ATT_BLOCK) == 0
    tiles = _tiles(s)
    row = lambda v: v.astype(F32).reshape(1, -1)
    bf = lambda w: w.astype(BF16)
    w_gate, w_up, w_down = bf(ffn_w_gate), bf(ffn_w_up), bf(ffn_w_down)

    q, k, v, u = _even_in(x, positions, row(norm_mix_pre[0]), bf(ev_w_in[0]), tiles["even_in_rows"])
    att = _attention(q, k, v)
    yg = _s5(u, *_s5_params(s5_a_re[0], s5_a_im[0], s5_log_dt[0], s5_b_re[0], s5_b_im[0],
                            s5_c_re[0], s5_c_im[0], s5_d[0]), tiles["s5_steps"])
    x1 = _even_post(att, yg.reshape(b * s, SSM_WIDTH), x.reshape(b * s, d),
                    bf(s5_w_glu[0]), bf(ev_w_out[0]), row(norm_mix_post[0]), tiles["post_rows"])
    x2 = _ffn(x1, row(norm_ffn_pre[0]), w_gate, w_up, w_down, row(norm_ffn_post[0]), 0, tiles["ffn_rows"])

    x3 = _rglru(x2.reshape(b, s, d), row(norm_mix_pre[1]), od_w_in[0], rg_conv_w[0].astype(F32),
                row(rg_conv_b[0]), bf(rg_w_r[0]), row(rg_b_r[0]), bf(rg_w_i[0]), row(rg_b_i[0]),
                row(rg_lam[0]), bf(od_w_out[0]), row(norm_mix_post[1]), tiles["rglru_steps"])
    x4 = _ffn(x3.reshape(b * s, d), row(norm_ffn_pre[1]), w_gate, w_up, w_down, row(norm_ffn_post[1]), 1,
              tiles["ffn_rows"])
    return x4.reshape(b, s, d)
```
